```python
import math
import jax, jax.numpy as jnp
from jax import lax
import numpy as np

D_MODEL = 1024
BATCH = 16
SEQ = 4096
DEPTH = 2
DEC_BATCH = 32
DEC_SEQ = 16
PAST_LEN = 2048

CHUNK = 64
N_A_LAYERS = DEPTH // 2
N_B_LAYERS = DEPTH - N_A_LAYERS
MLSTM_INNER = 2 * D_MODEL
MLSTM_HEADS = 4
MLSTM_HEAD_DIM = MLSTM_INNER // MLSTM_HEADS
QKV_BLOCK = 4
CONV_WIDTH = 4
MLA_HEADS = 8
QK_NOPE_DIM = 128
QK_ROPE_DIM = 64
V_HEAD_DIM = 128
KV_LORA_RANK = 256
Q_LORA_RANK = 384
MLA_GATE_WIDTH = MLA_HEADS * V_HEAD_DIM
MLA_SCALE = 1.0 / math.sqrt(QK_NOPE_DIM + QK_ROPE_DIM)
ROPE_BASE = 10000.0
Q_BLOCK = 128
EPS = 1e-6

kernel_name = "yoco_mlstm_mla_stream_step"


def rmsnorm(x, w):
    xf = x.astype(jnp.float32)
    y = xf * lax.rsqrt(jnp.mean(xf * xf, axis=-1, keepdims=True) + EPS)
    return (y * w.astype(jnp.float32)).astype(x.dtype)


def rope(x, pos):
    half = x.shape[-1] // 2
    inv = ROPE_BASE ** (-jnp.arange(half, dtype=jnp.float32) / half)
    ang = pos.astype(jnp.float32)[:, None] * inv[None, :]
    cos = jnp.cos(ang)[None, :, None, :]
    sin = jnp.sin(ang)[None, :, None, :]
    xf = x.astype(jnp.float32)
    x1, x2 = xf[..., :half], xf[..., half:]
    return jnp.concatenate([x1 * cos - x2 * sin, x1 * sin + x2 * cos], axis=-1).astype(x.dtype)


def causal_conv(x, buf, w, b):
    T = x.shape[1]
    xp = jnp.concatenate([buf.astype(x.dtype), x], axis=1)
    y = b + xp[:, 0:T] * w[0]
    for j in range(1, CONV_WIDTH):
        y = y + xp[:, j:j + T] * w[j]
    return y, xp[:, xp.shape[1] - (CONV_WIDTH - 1):]


def headwise(x, w):
    B, T, _ = x.shape
    xb = x.reshape(B, T, -1, QKV_BLOCK)
    return jnp.einsum('btgi,gio->btgo', xb, w).reshape(B, T, -1)


def mlstm_chunk_step(carry, blk):
    C, n, m = carry
    q, k, v, ig, lf = blk
    L = q.shape[2]
    b = jnp.cumsum(lf, axis=-1)
    causal = jnp.tril(jnp.ones((L, L), dtype=bool))
    d_log = jnp.where(causal, b[..., :, None] - b[..., None, :] + ig[..., None, :], -jnp.inf)
    inter_log = b + m[..., None]
    m_t = jnp.maximum(inter_log, jnp.max(d_log, axis=-1))
    w_intra = jnp.exp(d_log - m_t[..., None])
    w_inter = jnp.exp(inter_log - m_t)
    s = jnp.einsum('bhtd,bhsd->bhts', q, k) * w_intra
    num = w_inter[..., None] * jnp.einsum('bhvk,bhtk->bhtv', C, q) + jnp.einsum('bhts,bhsv->bhtv', s, v)
    den = w_inter * jnp.einsum('bhk,bhtk->bht', n, q) + jnp.sum(s, axis=-1)
    h = num / jnp.maximum(jnp.abs(den), jnp.exp(-m_t))[..., None]
    b_last = b[..., -1]
    w_log = b_last[..., None] - b + ig
    m_new = jnp.maximum(b_last + m, jnp.max(w_log, axis=-1))
    decay = jnp.exp(b_last + m - m_new)
    w_upd = jnp.exp(w_log - m_new[..., None])
    C_new = decay[..., None, None] * C + jnp.einsum('bhsv,bhsk->bhvk', v * w_upd[..., None], k)
    n_new = decay[..., None] * n + jnp.einsum('bhs,bhsk->bhk', w_upd, k)
    return (C_new, n_new, m_new), h


def mlstm_scan(q, k, v, ig, lf, C0, n0, m0):
    B, H, T, d = q.shape
    L = min(CHUNK, T)
    nc = T // L

    def blocks(a):
        return jnp.moveaxis(a.reshape(a.shape[:2] + (nc, L) + a.shape[3:]), 2, 0)

    (C, n, m), h = lax.scan(mlstm_chunk_step, (C0, n0, m0), (blocks(q), blocks(k), blocks(v), blocks(ig), blocks(lf)))
    h = jnp.moveaxis(h, 0, 2).reshape(B, H, T, d)
    return h, C, n, m


def mlstm_layer(x, C0, n0, m0, buf, norm_w, w_in, conv_w, conv_b, w_q, w_k, w_v, w_gates, b_gates, outnorm_w, skip, w_out):
    B, T, _ = x.shape
    H, d = MLSTM_HEADS, MLSTM_HEAD_DIM
    u = rmsnorm(x, norm_w) @ w_in
    x_in, z = u[..., :MLSTM_INNER], u[..., MLSTM_INNER:]
    xc_pre, new_buf = causal_conv(x_in, buf, conv_w, conv_b)
    xc = jax.nn.silu(xc_pre)
    q = headwise(xc, w_q)
    k = headwise(xc, w_k)
    v = headwise(x_in, w_v)
    g = (q @ w_gates[:MLSTM_INNER] + k @ w_gates[MLSTM_INNER:2 * MLSTM_INNER]
         + v @ w_gates[2 * MLSTM_INNER:] + b_gates).astype(jnp.float32)
    ig = jnp.transpose(g[..., :H], (0, 2, 1))
    lf = jnp.transpose(jax.nn.log_sigmoid(g[..., H:]), (0, 2, 1))

    def heads(a):
        return jnp.transpose(a.astype(jnp.float32).reshape(B, T, H, d), (0, 2, 1, 3))

    h, C, n, m = mlstm_scan(heads(q), heads(k) / math.sqrt(d), heads(v), ig, lf,
                            C0.astype(jnp.float32), n0.astype(jnp.float32), m0.astype(jnp.float32))
    mu = jnp.mean(h, axis=-1, keepdims=True)
    var = jnp.mean(jnp.square(h - mu), axis=-1, keepdims=True)
    hn = (h - mu) * lax.rsqrt(var + EPS)
    hn = jnp.transpose(hn, (0, 2, 1, 3)).reshape(B, T, MLSTM_INNER) * outnorm_w.astype(jnp.float32)
    y = (hn.astype(x.dtype) + skip * xc) * jax.nn.silu(z)
    return x + y @ w_out, C, n, m, new_buf


def shared_latent_kv(s, pos, kv_norm_w, kv_w_dkv, kv_latent_norm_w):
    u = rmsnorm(s, kv_norm_w) @ kv_w_dkv
    c_kv = rmsnorm(u[..., :KV_LORA_RANK], kv_latent_norm_w)
    k_pe = rope(u[..., None, KV_LORA_RANK:], pos)[:, :, 0]
    return c_kv, k_pe


def mla_block_attend(q_lat, q_pe, q_pos, c_kv, k_pe, k_pos):
    s = (jnp.einsum('bthr,bsr->bhts', q_lat, c_kv) + jnp.einsum('bthp,bsp->bhts', q_pe, k_pe)).astype(jnp.float32) * MLA_SCALE
    visible = (k_pos[None, :] // CHUNK) <= (q_pos[:, None] // CHUNK)
    s = jnp.where(visible[None, None], s, -jnp.inf)
    p = jax.nn.softmax(s, axis=-1).astype(c_kv.dtype)
    return jnp.einsum('bhts,bsr->bthr', p, c_kv)


def mla_attend(q_lat, q_pe, q_pos, c_kv, k_pe, k_pos):
    B, T = q_lat.shape[:2]
    if T > Q_BLOCK and T % Q_BLOCK == 0:
        nb = T // Q_BLOCK

        def split(a):
            return jnp.moveaxis(a.reshape((B, nb, Q_BLOCK) + a.shape[2:]), 1, 0)

        o = lax.map(lambda blk: mla_block_attend(blk[0], blk[1], blk[2], c_kv, k_pe, k_pos),
                    (split(q_lat), split(q_pe), q_pos.reshape(nb, Q_BLOCK)))
        return jnp.moveaxis(o, 0, 1).reshape((B, T) + o.shape[3:])
    return mla_block_attend(q_lat, q_pe, q_pos, c_kv, k_pe, k_pos)


def mla_layer(x, q_pos, c_kv, k_pe, k_pos, norm_w, w_in, q_norm_w, w_uq, w_out, kv_w_uk, kv_w_uv):
    B, T, _ = x.shape
    u = rmsnorm(x, norm_w) @ w_in
    cq = rmsnorm(u[..., :Q_LORA_RANK], q_norm_w)
    gate = u[..., Q_LORA_RANK:]
    q = (cq @ w_uq).reshape(B, T, MLA_HEADS, QK_NOPE_DIM + QK_ROPE_DIM)
    q_nope = q[..., :QK_NOPE_DIM]
    q_pe = rope(q[..., QK_NOPE_DIM:], q_pos)
    q_lat = jnp.einsum('bthn,rhn->bthr', q_nope, kv_w_uk)
    o_lat = mla_attend(q_lat, q_pe, q_pos, c_kv, k_pe, k_pos)
    o = jnp.einsum('bthr,rhv->bthv', o_lat, kv_w_uv).reshape(B, T, MLA_GATE_WIDTH)
    return x + (o * jax.nn.silu(gate)) @ w_out


def trunk(x, pos, C0, n0, m0, buf0, past_ckv, past_kpe,
          a_norm_w, a_w_in, a_conv_w, a_conv_b, a_w_q, a_w_k, a_w_v, a_w_gates, a_b_gates, a_outnorm_w, a_skip, a_w_out,
          kv_norm_w, kv_w_dkv, kv_latent_norm_w, kv_w_uk, kv_w_uv,
          b_norm_w, b_w_in, b_q_norm_w, b_w_uq, b_w_out, final_norm_w):
    new_C, new_n, new_m, new_buf = [], [], [], []
    c_kv = k_pe = keys_ckv = keys_kpe = k_pos = None
    for layer in range(DEPTH):
        if layer < N_A_LAYERS:
            l = layer
            x, C, n, m, buf = mlstm_layer(x, C0[l], n0[l], m0[l], buf0[l], a_norm_w[l], a_w_in[l], a_conv_w[l], a_conv_b[l],
                                          a_w_q[l], a_w_k[l], a_w_v[l], a_w_gates[l], a_b_gates[l], a_outnorm_w[l], a_skip[l], a_w_out[l])
            new_C.append(C)
            new_n.append(n)
            new_m.append(m)
            new_buf.append(buf)
            if layer == N_A_LAYERS - 1:
                c_kv, k_pe = shared_latent_kv(x, pos, kv_norm_w, kv_w_dkv, kv_latent_norm_w)
                keys_ckv = jnp.concatenate([past_ckv.astype(c_kv.dtype), c_kv], axis=1)
                keys_kpe = jnp.concatenate([past_kpe.astype(k_pe.dtype), k_pe], axis=1)
                k_pos = jnp.concatenate([jnp.arange(past_ckv.shape[1], dtype=jnp.int32), pos])
        else:
            l = layer - N_A_LAYERS
            x = mla_layer(x, pos, keys_ckv, keys_kpe, k_pos, b_norm_w[l], b_w_in[l], b_q_norm_w[l], b_w_uq[l], b_w_out[l],
                          kv_w_uk, kv_w_uv)
    y = rmsnorm(x, final_norm_w)
    return y, jnp.stack(new_C), jnp.stack(new_n), jnp.stack(new_m), jnp.stack(new_buf), c_kv, k_pe


def setup_inputs(seed: int = 0) -> dict:
    key = jax.random.key(seed)
    ks = jax.random.split(key, 40)
    f32 = jnp.float32
    nrm = lambda i, shape, scale: scale * jax.random.normal(ks[i], shape, f32)
    H, d = MLSTM_HEADS, MLSTM_HEAD_DIM
    return {
        "x_prompt": nrm(0, (BATCH, SEQ, D_MODEL), 1.0),
        "x_sample": nrm(1, (DEC_BATCH, DEC_SEQ, D_MODEL), 1.0),
        "state_mlstm_C": nrm(2, (N_A_LAYERS, DEC_BATCH, H, d, d), 0.05),
        "state_mlstm_n": nrm(3, (N_A_LAYERS, DEC_BATCH, H, d), 0.05),
        "state_mlstm_m": nrm(4, (N_A_LAYERS, DEC_BATCH, H), 0.5),
        "state_mlstm_conv": nrm(5, (N_A_LAYERS, DEC_BATCH, CONV_WIDTH - 1, MLSTM_INNER), 1.0),
        "cache_mla_ckv": nrm(6, (DEC_BATCH, PAST_LEN, KV_LORA_RANK), 1.0),
        "cache_mla_kpe": nrm(7, (DEC_BATCH, PAST_LEN, QK_ROPE_DIM), 1.0),
        "a_norm_w": 1.0 + nrm(8, (N_A_LAYERS, D_MODEL), 0.02),
        "a_w_in": nrm(9, (N_A_LAYERS, D_MODEL, 2 * MLSTM_INNER), D_MODEL ** -0.5),
        "a_conv_w": nrm(10, (N_A_LAYERS, CONV_WIDTH, MLSTM_INNER), CONV_WIDTH ** -0.5),
        "a_conv_b": nrm(11, (N_A_LAYERS, MLSTM_INNER), 0.02),
        "a_w_q": nrm(12, (N_A_LAYERS, MLSTM_INNER // QKV_BLOCK, QKV_BLOCK, QKV_BLOCK), QKV_BLOCK ** -0.5),
        "a_w_k": nrm(13, (N_A_LAYERS, MLSTM_INNER // QKV_BLOCK, QKV_BLOCK, QKV_BLOCK), QKV_BLOCK ** -0.5),
        "a_w_v": nrm(14, (N_A_LAYERS, MLSTM_INNER // QKV_BLOCK, QKV_BLOCK, QKV_BLOCK), QKV_BLOCK ** -0.5),
        "a_w_gates": nrm(15, (N_A_LAYERS, 3 * MLSTM_INNER, 2 * H), (3 * MLSTM_INNER) ** -0.5),
        "a_b_gates": jnp.concatenate([nrm(16, (N_A_LAYERS, H), 0.1),
                                      jax.random.uniform(ks[17], (N_A_LAYERS, H), f32, 3.0, 6.0)], axis=-1),
        "a_outnorm_w": 1.0 + nrm(18, (N_A_LAYERS, MLSTM_INNER), 0.02),
        "a_skip": 1.0 + nrm(19, (N_A_LAYERS, MLSTM_INNER), 0.02),
        "a_w_out": nrm(20, (N_A_LAYERS, MLSTM_INNER, D_MODEL), MLSTM_INNER ** -0.5),
        "kv_norm_w": 1.0 + nrm(21, (D_MODEL,), 0.02),
        "kv_w_dkv": nrm(22, (D_MODEL, KV_LORA_RANK + QK_ROPE_DIM), D_MODEL ** -0.5),
        "kv_latent_norm_w": 1.0 + nrm(23, (KV_LORA_RANK,), 0.02),
        "kv_w_uk": nrm(24, (KV_LORA_RANK, MLA_HEADS, QK_NOPE_DIM), KV_LORA_RANK ** -0.5),
        "kv_w_uv": nrm(25, (KV_LORA_RANK, MLA_HEADS, V_HEAD_DIM), KV_LORA_RANK ** -0.5),
        "b_norm_w": 1.0 + nrm(26, (N_B_LAYERS, D_MODEL), 0.02),
        "b_w_in": nrm(27, (N_B_LAYERS, D_MODEL, Q_LORA_RANK + MLA_GATE_WIDTH), D_MODEL ** -0.5),
        "b_q_norm_w": 1.0 + nrm(28, (N_B_LAYERS, Q_LORA_RANK), 0.02),
        "b_w_uq": nrm(29, (N_B_LAYERS, Q_LORA_RANK, MLA_HEADS * (QK_NOPE_DIM + QK_ROPE_DIM)), Q_LORA_RANK ** -0.5),
        "b_w_out": nrm(30, (N_B_LAYERS, MLA_GATE_WIDTH, D_MODEL), MLA_GATE_WIDTH ** -0.5),
        "final_norm_w": 1.0 + nrm(31, (D_MODEL,), 0.02),
    }


def reference(x_prompt, x_sample, state_mlstm_C, state_mlstm_n, state_mlstm_m, state_mlstm_conv, cache_mla_ckv, cache_mla_kpe,
              a_norm_w, a_w_in, a_conv_w, a_conv_b, a_w_q, a_w_k, a_w_v, a_w_gates, a_b_gates, a_outnorm_w, a_skip, a_w_out,
              kv_norm_w, kv_w_dkv, kv_latent_norm_w, kv_w_uk, kv_w_uv,
              b_norm_w, b_w_in, b_q_norm_w, b_w_uq, b_w_out, final_norm_w):
    weights = (a_norm_w, a_w_in, a_conv_w, a_conv_b, a_w_q, a_w_k, a_w_v, a_w_gates, a_b_gates, a_outnorm_w, a_skip, a_w_out,
               kv_norm_w, kv_w_dkv, kv_latent_norm_w, kv_w_uk, kv_w_uv,
               b_norm_w, b_w_in, b_q_norm_w, b_w_uq, b_w_out, final_norm_w)
    H, d = MLSTM_HEADS, MLSTM_HEAD_DIM
    Bp, Tp = x_prompt.shape[0], x_prompt.shape[1]
    pos_p = jnp.arange(Tp, dtype=jnp.int32)
    y_prompt, p_C, p_n, p_m, p_conv, p_ckv, p_kpe = trunk(
        x_prompt, pos_p,
        jnp.zeros((N_A_LAYERS, Bp, H, d, d), jnp.float32),
        jnp.zeros((N_A_LAYERS, Bp, H, d), jnp.float32),
        jnp.zeros((N_A_LAYERS, Bp, H), jnp.float32),
        jnp.zeros((N_A_LAYERS, Bp, CONV_WIDTH - 1, MLSTM_INNER), x_prompt.dtype),
        jnp.zeros((Bp, 0, KV_LORA_RANK), x_prompt.dtype),
        jnp.zeros((Bp, 0, QK_ROPE_DIM), x_prompt.dtype),
        *weights)
    past_len = cache_mla_ckv.shape[1]
    pos_s = past_len + jnp.arange(x_sample.shape[1], dtype=jnp.int32)
    y_sample, s_C, s_n, s_m, s_conv, s_ckv, s_kpe = trunk(
        x_sample, pos_s, state_mlstm_C, state_mlstm_n, state_mlstm_m, state_mlstm_conv,
        cache_mla_ckv, cache_mla_kpe, *weights)
    return (y_prompt, y_sample, p_C, p_n, p_m, p_conv, p_ckv, p_kpe, s_C, s_n, s_m, s_conv, s_ckv, s_kpe)
```

```python
import functools
import math

import jax
import jax.numpy as jnp
from jax import lax
from jax.experimental import pallas as pl
from jax.experimental.pallas import tpu as pltpu

F32 = jnp.float32
BF16 = jnp.bfloat16

D_MODEL = 1024
INNER = 2 * D_MODEL
N_HEADS = 4
HEAD_DIM = INNER // N_HEADS
QKV_BLOCK = 4
CONV_WIDTH = 4
MLA_HEADS = 8
QK_NOPE = 128
QK_ROPE = 64
V_HEAD = 128
KV_RANK = 256
Q_RANK = 384
GATE_W = MLA_HEADS * V_HEAD
MLA_SCALE = 1.0 / math.sqrt(QK_NOPE + QK_ROPE)
ROPE_BASE = 10000.0
MASK_CHUNK = 64
EPS = 1e-6

LANE = 128
BD = 256
KCAT = KV_RANK + LANE
VMEM_LIMIT = 56 * 1024 * 1024

NEG_INF = float("-inf")


def _cparams(sem):
    return pltpu.CompilerParams(dimension_semantics=sem, vmem_limit_bytes=VMEM_LIMIT)


def _const_spec(shape):
    nd = len(shape)
    return pl.BlockSpec(shape, lambda *_: (0,) * nd, pipeline_mode=pl.Buffered(1))


def _rms(x, w):
    return x * lax.rsqrt(jnp.mean(x * x, axis=-1, keepdims=True) + EPS) * w


def _silu(x):
    return x / (1.0 + jnp.exp(-x))


def _dot(a, b):
    return jnp.dot(a, b, preferred_element_type=F32)


def _dot_nt(a, b):
    return lax.dot_general(a, b, (((1,), (1,)), ((), ())), preferred_element_type=F32)


def _dot_tn(a, b):
    return lax.dot_general(a, b, (((0,), (0,)), ((), ())), preferred_element_type=F32)


def _rope128(x, cos_t, sin_t):
    lane = lax.broadcasted_iota(jnp.int32, x.shape, 1)
    partner = jnp.where(lane < QK_ROPE // 2, pltpu.roll(x, LANE - QK_ROPE // 2, 1), pltpu.roll(x, QK_ROPE // 2, 1))
    return x * cos_t + partner * sin_t


def _front_kernel(x_ref, buf_ref, nw_ref, win_ref, cw_ref, cb_ref, bdq_ref, bdk_ref, bdv_ref,
                  wgq_ref, wgk_ref, wgv_ref, bg_ref,
                  q_ref, k_ref, v_ref, xc_ref, sz_ref, g_ref, tail_ref, xs_ref, *, tt):
    t = pl.program_id(1)
    halo = CONV_WIDTH - 1

    @pl.when(t == 0)
    def _():
        xs_ref[8 - halo:8, :] = buf_ref[0]

    xnb = _rms(x_ref[0], nw_ref[...]).astype(BF16)
    x_in = _dot(xnb, win_ref[:, :INNER])
    xs_ref[8:8 + tt, :] = x_in
    sz_ref[0] = _silu(_dot(xnb, win_ref[:, INNER:])).astype(BF16)

    cw = cw_ref[...]
    conv = cb_ref[...] + x_in * cw[halo:halo + 1]
    for j in range(halo):
        conv = conv + xs_ref[8 - halo + j:8 - halo + j + tt, :] * cw[j:j + 1]
    tail_ref[0] = xs_ref[8 + tt - halo:8 + tt, :]
    xs_ref[0:8, :] = xs_ref[tt:tt + 8, :]

    xcb = _silu(conv).astype(BF16)
    xib = x_in.astype(BF16)
    xc_ref[0] = xcb

    g = bg_ref[...] + jnp.zeros((tt, LANE), F32)
    for j in range(INNER // BD):
        sl = slice(j * BD, (j + 1) * BD)
        qj = _dot(xcb[:, sl], bdq_ref[j])
        kj = _dot(xcb[:, sl], bdk_ref[j])
        vj = _dot(xib[:, sl], bdv_ref[j])
        qjb, kjb, vjb = qj.astype(BF16), kj.astype(BF16), vj.astype(BF16)
        g = g + _dot(qjb, wgq_ref[sl, :]) + _dot(kjb, wgk_ref[sl, :]) + _dot(vjb, wgv_ref[sl, :])
        q_ref[0, :, sl] = qjb
        k_ref[0, :, sl] = (kj * (HEAD_DIM ** -0.5)).astype(BF16)
        v_ref[0, :, sl] = vjb
    g_ref[0] = g[:, :2 * N_HEADS]


def _front(x, buf, nw, win, cw, cb, bdq, bdk, bdv, wgq, wgk, wgv, bg, *, tt):
    B, T, _ = x.shape
    nt = T // tt
    act = jax.ShapeDtypeStruct((B, T, INNER), BF16)
    tok = lambda w: pl.BlockSpec((1, tt, w), lambda b, t: (b, t, 0))
    return pl.pallas_call(
        functools.partial(_front_kernel, tt=tt),
        grid=(B, nt),
        in_specs=[tok(D_MODEL),
                  pl.BlockSpec((1, CONV_WIDTH - 1, INNER), lambda b, t: (b, 0, 0)),
                  _const_spec((1, D_MODEL)), _const_spec((D_MODEL, 2 * INNER)),
                  _const_spec((CONV_WIDTH, INNER)), _const_spec((1, INNER)),
                  _const_spec((INNER // BD, BD, BD)), _const_spec((INNER // BD, BD, BD)),
                  _const_spec((INNER // BD, BD, BD)),
                  _const_spec((INNER, LANE)), _const_spec((INNER, LANE)), _const_spec((INNER, LANE)),
                  _const_spec((1, LANE))],
        out_specs=[tok(INNER), tok(INNER), tok(INNER), tok(INNER), tok(INNER), tok(2 * N_HEADS),
                   pl.BlockSpec((1, CONV_WIDTH - 1, INNER), lambda b, t: (b, 0, 0))],
        out_shape=[act, act, act, act, act,
                   jax.ShapeDtypeStruct((B, T, 2 * N_HEADS), F32),
                   jax.ShapeDtypeStruct((B, CONV_WIDTH - 1, INNER), F32)],
        scratch_shapes=[pltpu.VMEM((8 + tt, INNER), F32)],
        compiler_params=_cparams(("parallel", "arbitrary")),
        name="mlstm_front",
    )(x, buf, nw, win, cw, cb, bdq, bdk, bdv, wgq, wgk, wgv, bg)


def _gate_prep_kernel(gt_ref, r_ref, *, chunk):
    g = gt_ref[0]
    ig = g[:N_HEADS]
    fg = g[N_HEADS:]
    lf = jnp.minimum(fg, 0.0) - jnp.log(1.0 + jnp.exp(-jnp.abs(fg)))
    pos = lax.broadcasted_iota(jnp.int32, lf.shape, 1) & (chunk - 1)
    b = lf
    sh = 1
    while sh < chunk:
        b = b + jnp.where(pos >= sh, pltpu.roll(b, sh, 1), 0.0)
        sh *= 2
    a = ig - b
    cma = a
    sh = 1
    while sh < chunk:
        cma = jnp.maximum(cma, jnp.where(pos >= sh, pltpu.roll(cma, sh, 1), NEG_INF))
        sh *= 2
    zero = jnp.zeros_like(a[0:1])
    for h in range(N_HEADS):
        r_ref[0, h] = jnp.concatenate([a[h:h + 1], b[h:h + 1], cma[h:h + 1]] + [zero] * 5, axis=0)


def _gate_prep(gt, *, chunk):
    B, _, Tp = gt.shape
    return pl.pallas_call(
        functools.partial(_gate_prep_kernel, chunk=chunk),
        grid=(B,),
        in_specs=[pl.BlockSpec((1, 2 * N_HEADS, Tp), lambda b: (b, 0, 0))],
        out_specs=pl.BlockSpec((1, N_HEADS, 8, Tp), lambda b: (b, 0, 0, 0)),
        out_shape=jax.ShapeDtypeStruct((B, N_HEADS, 8, Tp), F32),
        compiler_params=_cparams(("parallel",)),
        name="mlstm_gate_prep",
    )(gt)


def _recur_kernel(*refs, chunk, has_state):
    if has_state:
        (q_ref, k_ref, v_ref, r_ref, rc_ref, c0_ref, n0_ref, m0_ref,
         hn_ref, c_out, n_out, m_out, c_sc, n_sc, m_sc) = refs
    else:
        (q_ref, k_ref, v_ref, r_ref, rc_ref,
         hn_ref, c_out, n_out, m_out, c_sc, n_sc, m_sc) = refs
    c = pl.program_id(2)
    L = chunk

    @pl.when(c == 0)
    def _():
        if has_state:
            c_sc[...] = c0_ref[0, 0]
            n_sc[...] = n0_ref[0, 0]
            m_sc[...] = m0_ref[0, 0]
        else:
            c_sc[...] = jnp.zeros_like(c_sc)
            n_sc[...] = jnp.zeros_like(n_sc)
            m_sc[...] = jnp.zeros_like(m_sc)

    q = q_ref[0]
    k = k_ref[0]
    v = v_ref[0]
    a_row = r_ref[0, 0, 0:1, 0:L]
    a_col = rc_ref[0, 0, :, 0:1]
    b_col = rc_ref[0, 0, :, 1:2]
    cma_col = rc_ref[0, 0, :, 2:3]
    m = m_sc[...]
    n = n_sc[...]

    big_m = jnp.maximum(m, cma_col)
    m_t = b_col + big_m
    ti = lax.broadcasted_iota(jnp.int32, (L, L), 0)
    si = lax.broadcasted_iota(jnp.int32, (L, L), 1)
    w_intra = jnp.exp(jnp.where(si <= ti, a_row - big_m, NEG_INF))
    w_inter = jnp.exp(m - big_m)

    s = _dot_nt(q, k) * w_intra
    qc = _dot_nt(q, c_sc[...].astype(BF16))
    num = w_inter * qc + _dot(s.astype(BF16), v)
    qf = q.astype(F32)
    den = w_inter * jnp.sum(qf * n, axis=-1, keepdims=True) + jnp.sum(s, axis=-1, keepdims=True)
    h = num / jnp.maximum(jnp.abs(den), jnp.exp(-m_t))

    mu = jnp.mean(h, axis=-1, keepdims=True)
    hc = h - mu
    var = jnp.mean(hc * hc, axis=-1, keepdims=True)
    hn_ref[0] = (hc * lax.rsqrt(var + EPS)).astype(BF16)

    a_max = cma_col[L - 1:L, :]
    b_last = b_col[L - 1:L, :]
    big_n = jnp.maximum(m, a_max)
    decay = jnp.exp(m - big_n)
    w_upd = jnp.exp(a_col - big_n)
    vw = (v.astype(F32) * w_upd).astype(BF16)
    c_sc[...] = decay * c_sc[...] + _dot_tn(vw, k)
    n_sc[...] = decay * n + jnp.sum(k.astype(F32) * w_upd, axis=0, keepdims=True)
    m_sc[...] = b_last + big_n

    @pl.when(c == pl.num_programs(2) - 1)
    def _():
        c_out[0, 0] = c_sc[...]
        n_out[0, 0] = n_sc[...]
        m_out[0, 0] = m_sc[...]


def _recurrence(q, k, v, r, rc, state, *, chunk):
    B, T, _ = q.shape
    nc = T // chunk
    d = HEAD_DIM
    has_state = state is not None
    qkv_spec = pl.BlockSpec((1, chunk, d), lambda b, h, c: (b, c, h))
    lr = r.shape[-1] // nc
    in_specs = [qkv_spec, qkv_spec, qkv_spec,
                pl.BlockSpec((1, 1, 8, lr), lambda b, h, c: (b, h, 0, c)),
                pl.BlockSpec((1, 1, chunk, 8), lambda b, h, c: (b, h, c, 0))]
    args = [q, k, v, r, rc]
    st_specs = [pl.BlockSpec((1, 1, d, d), lambda b, h, c: (b, h, 0, 0)),
                pl.BlockSpec((1, 1, 1, d), lambda b, h, c: (b, h, 0, 0)),
                pl.BlockSpec((1, 1, 1, 1), lambda b, h, c: (b, h, 0, 0))]
    if has_state:
        in_specs += st_specs
        args += list(state)
    return pl.pallas_call(
        functools.partial(_recur_kernel, chunk=chunk, has_state=has_state),
        grid=(B, N_HEADS, nc),
        in_specs=in_specs,
        out_specs=[qkv_spec] + st_specs,
        out_shape=[jax.ShapeDtypeStruct((B, T, INNER), BF16),
                   jax.ShapeDtypeStruct((B, N_HEADS, d, d), F32),
                   jax.ShapeDtypeStruct((B, N_HEADS, 1, d), F32),
                   jax.ShapeDtypeStruct((B, N_HEADS, 1, 1), F32)],
        scratch_shapes=[pltpu.VMEM((d, d), F32), pltpu.VMEM((1, d), F32), pltpu.VMEM((1, 1), F32)],
        compiler_params=_cparams(("parallel", "parallel", "arbitrary")),
        name="mlstm_recurrence",
    )(*args)


def _mid_kernel(hn_ref, xc_ref, sz_ref, x_ref, cos_ref, sin_ref,
                onw_ref, skip_ref, wout_ref, kvnw_ref, dkv_ref, latw_ref,
                bnw_ref, bwin_ref, qnw_ref, wuq_ref, wuk_ref,
                x1_ref, ckv_ref, kpe_ref, kcat_ref, qcat_ref, gate_ref):
    cos_t = cos_ref[...]
    sin_t = sin_ref[...]
    y = (hn_ref[0].astype(F32) * onw_ref[...] + skip_ref[...] * xc_ref[0].astype(F32)) * sz_ref[0].astype(F32)
    x1 = x_ref[0] + _dot(y.astype(BF16), wout_ref[...])
    x1_ref[0] = x1

    ukv = _dot(_rms(x1, kvnw_ref[...]).astype(BF16), dkv_ref[...])
    ckv = _rms(ukv[:, :KV_RANK], latw_ref[...])
    kpe = _rope128(ukv[:, KV_RANK:], cos_t, sin_t)
    ckv_ref[0] = ckv
    kpe_ref[0] = kpe[:, :QK_ROPE]
    kcat_ref[0, :, :KV_RANK] = ckv.astype(BF16)
    kcat_ref[0, :, KV_RANK:] = kpe.astype(BF16)

    ub = _dot(_rms(x1, bnw_ref[...]).astype(BF16), bwin_ref[...])
    gate_ref[0] = _silu(ub[:, Q_RANK:]).astype(BF16)
    cq = _rms(ub[:, :Q_RANK], qnw_ref[...]).astype(BF16)
    qq = _dot(cq, wuq_ref[...])
    nope_w = MLA_HEADS * QK_NOPE
    for h in range(MLA_HEADS):
        q_lat = _dot(qq[:, h * QK_NOPE:(h + 1) * QK_NOPE].astype(BF16), wuk_ref[h])
        q_pe = _rope128(qq[:, nope_w + h * LANE:nope_w + (h + 1) * LANE], cos_t, sin_t)
        qcat_ref[0, h, :, :KV_RANK] = q_lat.astype(BF16)
        qcat_ref[0, h, :, KV_RANK:] = q_pe.astype(BF16)


def _mid(hn, xc, sz, x, cos_t, sin_t, onw, skip, wout, kvnw, dkv, latw, bnw, bwin, qnw, wuq, wuk, *, tt):
    B, T, _ = x.shape
    nt = T // tt
    tok = lambda w: pl.BlockSpec((1, tt, w), lambda b, t: (b, t, 0))
    pos = pl.BlockSpec((tt, LANE), lambda b, t: (t, 0))
    return pl.pallas_call(
        _mid_kernel,
        grid=(B, nt),
        in_specs=[tok(INNER), tok(INNER), tok(INNER), tok(D_MODEL), pos, pos,
                  _const_spec((1, INNER)), _const_spec((1, INNER)), _const_spec((INNER, D_MODEL)),
                  _const_spec((1, D_MODEL)), _const_spec((D_MODEL, KCAT)), _const_spec((1, KV_RANK)),
                  _const_spec((1, D_MODEL)), _const_spec((D_MODEL, Q_RANK + GATE_W)), _const_spec((1, Q_RANK)),
                  _const_spec((Q_RANK, 2 * MLA_HEADS * LANE)), _const_spec((MLA_HEADS, QK_NOPE, KV_RANK))],
        out_specs=[tok(D_MODEL), tok(KV_RANK), tok(QK_ROPE), tok(KCAT),
                   pl.BlockSpec((1, MLA_HEADS, tt, KCAT), lambda b, t: (b, 0, t, 0)),
                   tok(GATE_W)],
        out_shape=[jax.ShapeDtypeStruct((B, T, D_MODEL), F32),
                   jax.ShapeDtypeStruct((B, T, KV_RANK), F32),
                   jax.ShapeDtypeStruct((B, T, QK_ROPE), F32),
                   jax.ShapeDtypeStruct((B, T, KCAT), BF16),
                   jax.ShapeDtypeStruct((B, MLA_HEADS, T, KCAT), BF16),
                   jax.ShapeDtypeStruct((B, T, GATE_W), BF16)],
        compiler_params=_cparams(("parallel", "parallel")),
        name="mid_proj",
    )(hn, xc, sz, x, cos_t, sin_t, onw, skip, wout, kvnw, dkv, latw, bnw, bwin, qnw, wuq, wuk)


def _attn_prompt_kernel(q_ref, k_ref, o_ref, m_sc, l_sc, acc_sc, *, tq):
    i = pl.program_id(1)
    rows = MLA_HEADS * tq
    q = q_ref[0].reshape(rows, KCAT)

    def scores(j):
        kj = k_ref[0, pl.ds(pl.multiple_of(j * tq, tq), tq), :]
        return kj, _dot_nt(q, kj) * MLA_SCALE

    kd, s = scores(i)
    r = lax.broadcasted_iota(jnp.int32, (rows, tq), 0)
    col = lax.broadcasted_iota(jnp.int32, (rows, tq), 1)
    visible = (col // MASK_CHUNK) <= ((r & (tq - 1)) // MASK_CHUNK)
    s = jnp.where(visible, s, NEG_INF)
    m0 = jnp.max(s, axis=-1, keepdims=True)
    p = jnp.exp(s - m0)
    m_sc[...] = m0
    l_sc[...] = jnp.sum(p, axis=-1, keepdims=True)
    acc_sc[...] = _dot(p.astype(BF16), kd[:, :KV_RANK])

    def body(j, carry):
        kj, s = scores(j)
        m_prev = m_sc[...]
        m_new = jnp.maximum(m_prev, jnp.max(s, axis=-1, keepdims=True))
        alpha = jnp.exp(m_prev - m_new)
        p = jnp.exp(s - m_new)
        l_sc[...] = alpha * l_sc[...] + jnp.sum(p, axis=-1, keepdims=True)
        acc_sc[...] = alpha * acc_sc[...] + _dot(p.astype(BF16), kj[:, :KV_RANK])
        m_sc[...] = m_new
        return carry

    lax.fori_loop(0, i, body, 0)
    o = acc_sc[...] / l_sc[...]
    o_ref[0] = o.astype(BF16).reshape(MLA_HEADS, tq, KV_RANK)


def _attn_prompt(qcat, kcat, *, tq):
    B, _, T, _ = qcat.shape
    rows = MLA_HEADS * tq
    return pl.pallas_call(
        functools.partial(_attn_prompt_kernel, tq=tq),
        grid=(B, T // tq),
        in_specs=[pl.BlockSpec((1, MLA_HEADS, tq, KCAT), lambda b, i: (b, 0, i, 0)),
                  pl.BlockSpec((1, T, KCAT), lambda b, i: (b, 0, 0))],
        out_specs=pl.BlockSpec((1, MLA_HEADS, tq, KV_RANK), lambda b, i: (b, 0, i, 0)),
        out_shape=jax.ShapeDtypeStruct((B, MLA_HEADS, T, KV_RANK), BF16),
        scratch_shapes=[pltpu.VMEM((rows, 1), F32), pltpu.VMEM((rows, 1), F32), pltpu.VMEM((rows, KV_RANK), F32)],
        compiler_params=_cparams(("parallel", "arbitrary")),
        name="mla_attention_prompt",
    )(qcat, kcat)


def _attn_cached_kernel(q_ref, k_ref, o_ref, *, tq, q_pos0, n_keys):
    rows = MLA_HEADS * tq
    q = q_ref[0].reshape(rows, KCAT)
    k = k_ref[0]
    s = _dot_nt(q, k) * MLA_SCALE
    r = lax.broadcasted_iota(jnp.int32, s.shape, 0)
    col = lax.broadcasted_iota(jnp.int32, s.shape, 1)
    q_pos = q_pos0 + (r & (tq - 1))
    visible = ((col // MASK_CHUNK) <= (q_pos // MASK_CHUNK)) & (col < n_keys)
    s = jnp.where(visible, s, NEG_INF)
    p = jnp.exp(s - jnp.max(s, axis=-1, keepdims=True))
    l = jnp.sum(p, axis=-1, keepdims=True)
    o = _dot(p.astype(BF16), k[:, :KV_RANK]) / l
    o_ref[0] = o.astype(BF16).reshape(MLA_HEADS, tq, KV_RANK)


def _attn_cached(qcat, kcat, *, q_pos0, n_keys):
    B, _, tq, _ = qcat.shape
    sp = kcat.shape[1]
    return pl.pallas_call(
        functools.partial(_attn_cached_kernel, tq=tq, q_pos0=q_pos0, n_keys=n_keys),
        grid=(B,),
        in_specs=[pl.BlockSpec((1, MLA_HEADS, tq, KCAT), lambda b: (b, 0, 0, 0)),
                  pl.BlockSpec((1, sp, KCAT), lambda b: (b, 0, 0))],
        out_specs=pl.BlockSpec((1, MLA_HEADS, tq, KV_RANK), lambda b: (b, 0, 0, 0)),
        out_shape=jax.ShapeDtypeStruct((B, MLA_HEADS, tq, KV_RANK), BF16),
        compiler_params=_cparams(("parallel",)),
        name="mla_attention_cached",
    )(qcat, kcat)


def _final_kernel(o_ref, gate_ref, x1_ref, wuv_ref, wout_ref, fnw_ref, y_ref):
    parts = [_dot(o_ref[0, h], wuv_ref[h]) for h in range(MLA_HEADS)]
    o = jnp.concatenate(parts, axis=-1)
    og = (o * gate_ref[0].astype(F32)).astype(BF16)
    x2 = x1_ref[0] + _dot(og, wout_ref[...])
    y_ref[0] = _rms(x2, fnw_ref[...])


def _final(o_lat, gate, x1, wuv, wout, fnw, *, tt):
    B, T, _ = x1.shape
    tok = lambda w: pl.BlockSpec((1, tt, w), lambda b, t: (b, t, 0))
    return pl.pallas_call(
        _final_kernel,
        grid=(B, T // tt),
        in_specs=[pl.BlockSpec((1, MLA_HEADS, tt, KV_RANK), lambda b, t: (b, 0, t, 0)),
                  tok(GATE_W), tok(D_MODEL),
                  _const_spec((MLA_HEADS, KV_RANK, V_HEAD)), _const_spec((GATE_W, D_MODEL)),
                  _const_spec((1, D_MODEL))],
        out_specs=tok(D_MODEL),
        out_shape=jax.ShapeDtypeStruct((B, T, D_MODEL), F32),
        compiler_params=_cparams(("parallel", "parallel")),
        name="mla_final",
    )(o_lat, gate, x1, wuv, wout, fnw)


def _block_diag(w):
    g = BD // QKV_BLOCK
    nb = INNER // BD
    wb = w.reshape(nb, g, QKV_BLOCK, QKV_BLOCK)
    eye = jnp.eye(g, dtype=w.dtype)
    dense = wb[:, :, :, None, :] * eye[None, :, None, :, None]
    return dense.reshape(nb, BD, BD).astype(BF16)


def _pad_lanes(w, width):
    return jnp.pad(w, ((0, 0),) * (w.ndim - 1) + ((0, width - w.shape[-1]),))


def _prep_weights(a_norm_w, a_w_in, a_conv_w, a_conv_b, a_w_q, a_w_k, a_w_v, a_w_gates, a_b_gates, a_outnorm_w,
                  a_skip, a_w_out, kv_norm_w, kv_w_dkv, kv_latent_norm_w, kv_w_uk, kv_w_uv,
                  b_norm_w, b_w_in, b_q_norm_w, b_w_uq, b_w_out, final_norm_w):
    l = 0
    wg = a_w_gates[l]
    wuq = b_w_uq[l].reshape(Q_RANK, MLA_HEADS, QK_NOPE + QK_ROPE)
    wuq_nope = wuq[:, :, :QK_NOPE].reshape(Q_RANK, MLA_HEADS * QK_NOPE)
    wuq_rope = jnp.pad(wuq[:, :, QK_NOPE:], ((0, 0), (0, 0), (0, LANE - QK_ROPE))).reshape(Q_RANK, MLA_HEADS * LANE)
    return dict(
        nw=a_norm_w[l][None], win=a_w_in[l].astype(BF16), cw=a_conv_w[l], cb=a_conv_b[l][None],
        bdq=_block_diag(a_w_q[l]), bdk=_block_diag(a_w_k[l]), bdv=_block_diag(a_w_v[l]),
        wgq=_pad_lanes(wg[:INNER], LANE).astype(BF16), wgk=_pad_lanes(wg[INNER:2 * INNER], LANE).astype(BF16),
        wgv=_pad_lanes(wg[2 * INNER:], LANE).astype(BF16), bg=_pad_lanes(a_b_gates[l][None], LANE),
        onw=a_outnorm_w[l][None], skip=a_skip[l][None], wout_a=a_w_out[l].astype(BF16),
        kvnw=kv_norm_w[None], dkv=_pad_lanes(kv_w_dkv, KCAT).astype(BF16), latw=kv_latent_norm_w[None],
        bnw=b_norm_w[l][None], bwin=b_w_in[l].astype(BF16), qnw=b_q_norm_w[l][None],
        wuq=jnp.concatenate([wuq_nope, wuq_rope], axis=1).astype(BF16),
        wuk=jnp.transpose(kv_w_uk, (1, 2, 0)).astype(BF16),
        wuv=jnp.transpose(kv_w_uv, (1, 0, 2)).astype(BF16),
        wout_b=b_w_out[l].astype(BF16), fnw=final_norm_w[None],
    )


def _rope_tables(pos):
    half = QK_ROPE // 2
    inv = ROPE_BASE ** (-jnp.arange(half, dtype=F32) / half)
    ang = pos.astype(F32)[:, None] * inv[None, :]
    cos, sin = jnp.cos(ang), jnp.sin(ang)
    zero = jnp.zeros((pos.shape[0], LANE - QK_ROPE), F32)
    return jnp.concatenate([cos, cos, zero], axis=1), jnp.concatenate([-sin, sin, zero], axis=1)


def _trunk(x, pos0, state, conv_buf, cache, w, *, tt, chunk):
    B, T, _ = x.shape
    q, k, v, xc, sz, g, conv_tail = _front(x, conv_buf, w["nw"], w["win"], w["cw"], w["cb"], w["bdq"], w["bdk"],
                                           w["bdv"], w["wgq"], w["wgk"], w["wgv"], w["bg"], tt=tt)
    gt = jnp.transpose(g, (0, 2, 1))
    tp = max(T, LANE)
    if tp != T:
        gt = jnp.pad(gt, ((0, 0), (0, 0), (0, tp - T)))
    r = _gate_prep(gt, chunk=chunk)
    rc = jnp.transpose(r[..., :T], (0, 1, 3, 2))
    if state is not None:
        c0, n0, m0 = state
        state = (c0, n0[:, :, None, :], m0[:, :, None, None])
    hn, c_new, n_new, m_new = _recurrence(q, k, v, r, rc, state, chunk=chunk)

    cos_t, sin_t = _rope_tables(pos0 + jnp.arange(T, dtype=jnp.int32))
    x1, ckv, kpe, kcat, qcat, gate = _mid(hn, xc, sz, x, cos_t, sin_t, w["onw"], w["skip"], w["wout_a"], w["kvnw"],
                                          w["dkv"], w["latw"], w["bnw"], w["bwin"], w["qnw"], w["wuq"], w["wuk"],
                                          tt=tt)
    if cache is None:
        o_lat = _attn_prompt(qcat, kcat, tq=tt)
    else:
        past_ckv, past_kpe = cache
        past = past_ckv.shape[1]
        n_keys = past + T
        sp = -(-n_keys // LANE) * LANE
        past_cat = jnp.concatenate([past_ckv.astype(BF16), _pad_lanes(past_kpe, LANE).astype(BF16)], axis=-1)
        keys = jnp.concatenate([past_cat, kcat, jnp.zeros((B, sp - n_keys, KCAT), BF16)], axis=1)
        o_lat = _attn_cached(qcat, keys, q_pos0=pos0, n_keys=n_keys)
    y = _final(o_lat, gate, x1, w["wuv"], w["wout_b"], w["fnw"], tt=tt)
    return (y, c_new[None], n_new[None, :, :, 0, :], m_new[None, :, :, 0, 0], conv_tail[None], ckv, kpe)


def kernel(x_prompt, x_sample, state_mlstm_C, state_mlstm_n, state_mlstm_m, state_mlstm_conv, cache_mla_ckv,
           cache_mla_kpe, a_norm_w, a_w_in, a_conv_w, a_conv_b, a_w_q, a_w_k, a_w_v, a_w_gates, a_b_gates,
           a_outnorm_w, a_skip, a_w_out, kv_norm_w, kv_w_dkv, kv_latent_norm_w, kv_w_uk, kv_w_uv, b_norm_w, b_w_in,
           b_q_norm_w, b_w_uq, b_w_out, final_norm_w):
    w = _prep_weights(a_norm_w, a_w_in, a_conv_w, a_conv_b, a_w_q, a_w_k, a_w_v, a_w_gates, a_b_gates, a_outnorm_w,
                      a_skip, a_w_out, kv_norm_w, kv_w_dkv, kv_latent_norm_w, kv_w_uk, kv_w_uv,
                      b_norm_w, b_w_in, b_q_norm_w, b_w_uq, b_w_out, final_norm_w)
    bp, tp, _ = x_prompt.shape
    bs, ts, _ = x_sample.shape
    past = cache_mla_ckv.shape[1]
    zero_buf = jnp.zeros((bp, CONV_WIDTH - 1, INNER), F32)
    out_p = _trunk(x_prompt, 0, None, zero_buf, None, w, tt=256, chunk=256)
    out_s = _trunk(x_sample, past, (state_mlstm_C[0], state_mlstm_n[0], state_mlstm_m[0]), state_mlstm_conv[0],
                   (cache_mla_ckv, cache_mla_kpe), w, tt=ts, chunk=ts)
    return (out_p[0], out_s[0]) + tuple(out_p[1:]) + tuple(out_s[1:])
```

```python
import functools
import math

import jax
import jax.numpy as jnp
from jax import lax
from jax.experimental import pallas as pl
from jax.experimental.pallas import tpu as pltpu

F32 = jnp.float32
BF16 = jnp.bfloat16

D_MODEL = 1024
INNER = 2 * D_MODEL
N_HEADS = 4
HEAD_DIM = INNER // N_HEADS
QKV_BLOCK = 4
CONV_WIDTH = 4
MLA_HEADS = 8
QK_NOPE = 128
QK_ROPE = 64
V_HEAD = 128
KV_RANK = 256
Q_RANK = 384
GATE_W = MLA_HEADS * V_HEAD
MLA_SCALE = 1.0 / math.sqrt(QK_NOPE + QK_ROPE)
ROPE_BASE = 10000.0
MASK_CHUNK = 64
EPS = 1e-6

LANE = 128
BD = 256
KCAT = KV_RANK + LANE
VMEM_LIMIT = 56 * 1024 * 1024

NEG_INF = float("-inf")


def _cparams(sem):
    return pltpu.CompilerParams(dimension_semantics=sem, vmem_limit_bytes=VMEM_LIMIT)


def _const_spec(shape):
    nd = len(shape)
    return pl.BlockSpec(shape, lambda *_: (0,) * nd, pipeline_mode=pl.Buffered(1))


def _rms(x, w):
    return x * lax.rsqrt(jnp.mean(x * x, axis=-1, keepdims=True) + EPS) * w


def _silu(x):
    return x / (1.0 + jnp.exp(-x))


def _dot(a, b):
    return jnp.dot(a, b, preferred_element_type=F32)


def _dot_nt(a, b):
    return lax.dot_general(a, b, (((1,), (1,)), ((), ())), preferred_element_type=F32)


def _dot_tn(a, b):
    return lax.dot_general(a, b, (((0,), (0,)), ((), ())), preferred_element_type=F32)


def _rope128(x, cos_t, sin_t):
    lane = lax.broadcasted_iota(jnp.int32, x.shape, 1)
    partner = jnp.where(lane < QK_ROPE // 2, pltpu.roll(x, LANE - QK_ROPE // 2, 1), pltpu.roll(x, QK_ROPE // 2, 1))
    return x * cos_t + partner * sin_t


def _front_kernel(x_ref, buf_ref, nw_ref, win_ref, cw_ref, cb_ref, bdq_ref, bdk_ref, bdv_ref,
                  wgq_ref, wgk_ref, wgv_ref, bg_ref,
                  q_ref, k_ref, v_ref, xc_ref, sz_ref, g_ref, tail_ref, xs_ref, *, tt):
    t = pl.program_id(1)
    halo = CONV_WIDTH - 1

    @pl.when(t == 0)
    def _():
        xs_ref[8 - halo:8, :] = buf_ref[0]

    xnb = _rms(x_ref[0], nw_ref[...]).astype(BF16)
    x_in = _dot(xnb, win_ref[:, :INNER])
    xs_ref[8:8 + tt, :] = x_in
    sz_ref[0] = _silu(_dot(xnb, win_ref[:, INNER:])).astype(BF16)

    cw = cw_ref[...]
    conv = cb_ref[...] + x_in * cw[halo:halo + 1]
    for j in range(halo):
        conv = conv + xs_ref[8 - halo + j:8 - halo + j + tt, :] * cw[j:j + 1]
    tail_ref[0] = xs_ref[8 + tt - halo:8 + tt, :]
    xs_ref[0:8, :] = xs_ref[tt:tt + 8, :]

    xcb = _silu(conv).astype(BF16)
    xib = x_in.astype(BF16)
    xc_ref[0] = xcb

    g = bg_ref[...] + jnp.zeros((tt, LANE), F32)
    for j in range(INNER // BD):
        sl = slice(j * BD, (j + 1) * BD)
        qj = _dot(xcb[:, sl], bdq_ref[j])
        kj = _dot(xcb[:, sl], bdk_ref[j])
        vj = _dot(xib[:, sl], bdv_ref[j])
        qjb, kjb, vjb = qj.astype(BF16), kj.astype(BF16), vj.astype(BF16)
        g = g + _dot(qjb, wgq_ref[sl, :]) + _dot(kjb, wgk_ref[sl, :]) + _dot(vjb, wgv_ref[sl, :])
        q_ref[0, :, sl] = qjb
        k_ref[0, :, sl] = (kj * (HEAD_DIM ** -0.5)).astype(BF16)
        v_ref[0, :, sl] = vjb
    g_ref[0] = g[:, :2 * N_HEADS]


def _front(x, buf, nw, win, cw, cb, bdq, bdk, bdv, wgq, wgk, wgv, bg, *, tt):
    B, T, _ = x.shape
    nt = T // tt
    act = jax.ShapeDtypeStruct((B, T, INNER), BF16)
    tok = lambda w: pl.BlockSpec((1, tt, w), lambda b, t: (b, t, 0))
    return pl.pallas_call(
        functools.partial(_front_kernel, tt=tt),
        grid=(B, nt),
        in_specs=[tok(D_MODEL),
                  pl.BlockSpec((1, CONV_WIDTH - 1, INNER), lambda b, t: (b, 0, 0)),
                  _const_spec((1, D_MODEL)), _const_spec((D_MODEL, 2 * INNER)),
                  _const_spec((CONV_WIDTH, INNER)), _const_spec((1, INNER)),
                  _const_spec((INNER // BD, BD, BD)), _const_spec((INNER // BD, BD, BD)),
                  _const_spec((INNER // BD, BD, BD)),
                  _const_spec((INNER, LANE)), _const_spec((INNER, LANE)), _const_spec((INNER, LANE)),
                  _const_spec((1, LANE))],
        out_specs=[tok(INNER), tok(INNER), tok(INNER), tok(INNER), tok(INNER), tok(2 * N_HEADS),
                   pl.BlockSpec((1, CONV_WIDTH - 1, INNER), lambda b, t: (b, 0, 0))],
        out_shape=[act, act, act, act, act,
                   jax.ShapeDtypeStruct((B, T, 2 * N_HEADS), F32),
                   jax.ShapeDtypeStruct((B, CONV_WIDTH - 1, INNER), F32)],
        scratch_shapes=[pltpu.VMEM((8 + tt, INNER), F32)],
        compiler_params=_cparams(("parallel", "arbitrary")),
        name="mlstm_front",
    )(x, buf, nw, win, cw, cb, bdq, bdk, bdv, wgq, wgk, wgv, bg)


def _gate_prep_kernel(gt_ref, r_ref, *, chunk):
    g = gt_ref[0]
    ig = g[:N_HEADS]
    fg = g[N_HEADS:]
    lf = jnp.minimum(fg, 0.0) - jnp.log(1.0 + jnp.exp(-jnp.abs(fg)))
    pos = lax.broadcasted_iota(jnp.int32, lf.shape, 1) & (chunk - 1)
    b = lf
    sh = 1
    while sh < chunk:
        b = b + jnp.where(pos >= sh, pltpu.roll(b, sh, 1), 0.0)
        sh *= 2
    a = ig - b
    cma = a
    sh = 1
    while sh < chunk:
        cma = jnp.maximum(cma, jnp.where(pos >= sh, pltpu.roll(cma, sh, 1), NEG_INF))
        sh *= 2
    zero = jnp.zeros_like(a[0:1])
    for h in range(N_HEADS):
        r_ref[0, h] = jnp.concatenate([a[h:h + 1], b[h:h + 1], cma[h:h + 1]] + [zero] * 5, axis=0)


def _gate_prep(gt, *, chunk):
    B, _, Tp = gt.shape
    return pl.pallas_call(
        functools.partial(_gate_prep_kernel, chunk=chunk),
        grid=(B,),
        in_specs=[pl.BlockSpec((1, 2 * N_HEADS, Tp), lambda b: (b, 0, 0))],
        out_specs=pl.BlockSpec((1, N_HEADS, 8, Tp), lambda b: (b, 0, 0, 0)),
        out_shape=jax.ShapeDtypeStruct((B, N_HEADS, 8, Tp), F32),
        compiler_params=_cparams(("parallel",)),
        name="mlstm_gate_prep",
    )(gt)


def _recur_kernel(*refs, chunk, has_state):
    if has_state:
        (q_ref, k_ref, v_ref, r_ref, rc_ref, c0_ref, n0_ref, m0_ref,
         hn_ref, c_out, n_out, m_out, c_sc, n_sc, m_sc) = refs
    else:
        (q_ref, k_ref, v_ref, r_ref, rc_ref,
         hn_ref, c_out, n_out, m_out, c_sc, n_sc, m_sc) = refs
    c = pl.program_id(2)
    L = chunk

    @pl.when(c == 0)
    def _():
        if has_state:
            c_sc[...] = c0_ref[0, 0]
            n_sc[...] = n0_ref[0, 0]
            m_sc[...] = m0_ref[0, 0]
        else:
            c_sc[...] = jnp.zeros_like(c_sc)
            n_sc[...] = jnp.zeros_like(n_sc)
            m_sc[...] = jnp.zeros_like(m_sc)

    q = q_ref[0]
    k = k_ref[0]
    v = v_ref[0]
    a_row = r_ref[0, 0, 0:1, 0:L]
    a_col = rc_ref[0, 0, :, 0:1]
    b_col = rc_ref[0, 0, :, 1:2]
    cma_col = rc_ref[0, 0, :, 2:3]
    m = m_sc[...]
    n = n_sc[...]

    big_m = jnp.maximum(m, cma_col)
    m_t = b_col + big_m
    ti = lax.broadcasted_iota(jnp.int32, (L, L), 0)
    si = lax.broadcasted_iota(jnp.int32, (L, L), 1)
    w_intra = jnp.exp(jnp.where(si <= ti, a_row - big_m, NEG_INF))
    w_inter = jnp.exp(m - big_m)

    s = _dot_nt(q, k) * w_intra
    qc = _dot_nt(q, c_sc[...].astype(BF16))
    num = w_inter * qc + _dot(s.astype(BF16), v)
    qf = q.astype(F32)
    den = w_inter * jnp.sum(qf * n, axis=-1, keepdims=True) + jnp.sum(s, axis=-1, keepdims=True)
    h = num / jnp.maximum(jnp.abs(den), jnp.exp(-m_t))

    mu = jnp.mean(h, axis=-1, keepdims=True)
    hc = h - mu
    var = jnp.mean(hc * hc, axis=-1, keepdims=True)
    hn_ref[0] = (hc * lax.rsqrt(var + EPS)).astype(BF16)

    a_max = cma_col[L - 1:L, :]
    b_last = b_col[L - 1:L, :]
    big_n = jnp.maximum(m, a_max)
    decay = jnp.exp(m - big_n)
    w_upd = jnp.exp(a_col - big_n)
    vw = (v.astype(F32) * w_upd).astype(BF16)
    c_sc[...] = decay * c_sc[...] + _dot_tn(vw, k)
    n_sc[...] = decay * n + jnp.sum(k.astype(F32) * w_upd, axis=0, keepdims=True)
    m_sc[...] = b_last + big_n

    @pl.when(c == pl.num_programs(2) - 1)
    def _():
        c_out[0, 0] = c_sc[...]
        n_out[0, 0] = n_sc[...]
        m_out[0, 0] = m_sc[...]


def _recurrence(q, k, v, r, rc, state, *, chunk):
    B, T, _ = q.shape
    nc = T // chunk
    d = HEAD_DIM
    has_state = state is not None
    qkv_spec = pl.BlockSpec((1, chunk, d), lambda b, h, c: (b, c, h))
    lr = r.shape[-1] // nc
    in_specs = [qkv_spec, qkv_spec, qkv_spec,
                pl.BlockSpec((1, 1, 8, lr), lambda b, h, c: (b, h, 0, c)),
                pl.BlockSpec((1, 1, chunk, 8), lambda b, h, c: (b, h, c, 0))]
    args = [q, k, v, r, rc]
    st_specs = [pl.BlockSpec((1, 1, d, d), lambda b, h, c: (b, h, 0, 0)),
                pl.BlockSpec((1, 1, 1, d), lambda b, h, c: (b, h, 0, 0)),
                pl.BlockSpec((1, 1, 1, 1), lambda b, h, c: (b, h, 0, 0))]
    if has_state:
        in_specs += st_specs
        args += list(state)
    return pl.pallas_call(
        functools.partial(_recur_kernel, chunk=chunk, has_state=has_state),
        grid=(B, N_HEADS, nc),
        in_specs=in_specs,
        out_specs=[qkv_spec] + st_specs,
        out_shape=[jax.ShapeDtypeStruct((B, T, INNER), BF16),
                   jax.ShapeDtypeStruct((B, N_HEADS, d, d), F32),
                   jax.ShapeDtypeStruct((B, N_HEADS, 1, d), F32),
                   jax.ShapeDtypeStruct((B, N_HEADS, 1, 1), F32)],
        scratch_shapes=[pltpu.VMEM((d, d), F32), pltpu.VMEM((1, d), F32), pltpu.VMEM((1, 1), F32)],
        compiler_params=_cparams(("parallel", "parallel", "arbitrary")),
        name="mlstm_recurrence",
    )(*args)


def _mid_kernel(hn_ref, xc_ref, sz_ref, x_ref, cos_ref, sin_ref,
                onw_ref, skip_ref, wout_ref, kvnw_ref, dkv_ref, latw_ref,
                bnw_ref, bwin_ref, qnw_ref, wuq_ref, wuk_ref,
                x1_ref, ckv_ref, kpe_ref, kcat_ref, qcat_ref, gate_ref):
    cos_t = cos_ref[...]
    sin_t = sin_ref[...]
    y = (hn_ref[0].astype(F32) * onw_ref[...] + skip_ref[...] * xc_ref[0].astype(F32)) * sz_ref[0].astype(F32)
    x1 = x_ref[0] + _dot(y.astype(BF16), wout_ref[...])
    x1_ref[0] = x1

    ukv = _dot(_rms(x1, kvnw_ref[...]).astype(BF16), dkv_ref[...])
    ckv = _rms(ukv[:, :KV_RANK], latw_ref[...])
    kpe = _rope128(ukv[:, KV_RANK:], cos_t, sin_t)
    ckv_ref[0] = ckv
    kpe_ref[0] = kpe[:, :QK_ROPE]
    kcat_ref[0, :, :KV_RANK] = ckv.astype(BF16)
    kcat_ref[0, :, KV_RANK:] = kpe.astype(BF16)

    ub = _dot(_rms(x1, bnw_ref[...]).astype(BF16), bwin_ref[...])
    gate_ref[0] = _silu(ub[:, Q_RANK:]).astype(BF16)
    cq = _rms(ub[:, :Q_RANK], qnw_ref[...]).astype(BF16)
    qq = _dot(cq, wuq_ref[...])
    nope_w = MLA_HEADS * QK_NOPE
    for h in range(MLA_HEADS):
        q_lat = _dot(qq[:, h * QK_NOPE:(h + 1) * QK_NOPE].astype(BF16), wuk_ref[h])
        q_pe = _rope128(qq[:, nope_w + h * LANE:nope_w + (h + 1) * LANE], cos_t, sin_t)
        qcat_ref[0, h, :, :KV_RANK] = q_lat.astype(BF16)
        qcat_ref[0, h, :, KV_RANK:] = q_pe.astype(BF16)


def _mid(hn, xc, sz, x, cos_t, sin_t, onw, skip, wout, kvnw, dkv, latw, bnw, bwin, qnw, wuq, wuk, *, tt):
    B, T, _ = x.shape
    nt = T // tt
    tok = lambda w: pl.BlockSpec((1, tt, w), lambda b, t: (b, t, 0))
    pos = pl.BlockSpec((tt, LANE), lambda b, t: (t, 0))
    return pl.pallas_call(
        _mid_kernel,
        grid=(B, nt),
        in_specs=[tok(INNER), tok(INNER), tok(INNER), tok(D_MODEL), pos, pos,
                  _const_spec((1, INNER)), _const_spec((1, INNER)), _const_spec((INNER, D_MODEL)),
                  _const_spec((1, D_MODEL)), _const_spec((D_MODEL, KCAT)), _const_spec((1, KV_RANK)),
                  _const_spec((1, D_MODEL)), _const_spec((D_MODEL, Q_RANK + GATE_W)), _const_spec((1, Q_RANK)),
                  _const_spec((Q_RANK, 2 * MLA_HEADS * LANE)), _const_spec((MLA_HEADS, QK_NOPE, KV_RANK))],
        out_specs=[tok(D_MODEL), tok(KV_RANK), tok(QK_ROPE), tok(KCAT),
                   pl.BlockSpec((1, MLA_HEADS, tt, KCAT), lambda b, t: (b, 0, t, 0)),
                   tok(GATE_W)],
        out_shape=[jax.ShapeDtypeStruct((B, T, D_MODEL), F32),
                   jax.ShapeDtypeStruct((B, T, KV_RANK), F32),
                   jax.ShapeDtypeStruct((B, T, QK_ROPE), F32),
                   jax.ShapeDtypeStruct((B, T, KCAT), BF16),
                   jax.ShapeDtypeStruct((B, MLA_HEADS, T, KCAT), BF16),
                   jax.ShapeDtypeStruct((B, T, GATE_W), BF16)],
        compiler_params=_cparams(("parallel", "parallel")),
        name="mid_proj",
    )(hn, xc, sz, x, cos_t, sin_t, onw, skip, wout, kvnw, dkv, latw, bnw, bwin, qnw, wuq, wuk)


def _attn_prompt_kernel(q_ref, k_ref, vt_ref, o_ref, acc_sc, *, tq):
    i = pl.program_id(1)

    def step(b0, nb, m, l, first):
        tk = nb * tq
        kj = k_ref[0, pl.ds(pl.multiple_of(b0 * tq, tq), tk), :]
        vtj = jnp.concatenate([vt_ref[0, b0 + n] for n in range(nb)], axis=1) if nb > 1 else vt_ref[0, b0]
        if first:
            ks = lax.broadcasted_iota(jnp.int32, (tk, tq), 0) - (nb - 1) * tq
            qs = lax.broadcasted_iota(jnp.int32, (tk, tq), 1)
            visible = (ks < 0) | ((ks // MASK_CHUNK) <= (qs // MASK_CHUNK))
        m_out, l_out = [], []
        ahead = 3
        raw = [_dot_nt(kj, q_ref[0, h]) for h in range(ahead)]
        for h in range(MLA_HEADS):
            if h + ahead < MLA_HEADS:
                raw.append(_dot_nt(kj, q_ref[0, h + ahead]))
            s = raw[h] * MLA_SCALE
            if first:
                s = jnp.where(visible, s, NEG_INF)
                m_new = jnp.max(s, axis=0, keepdims=True)
                p = jnp.exp(s - m_new)
                acc_sc[h] = _dot(vtj, p.astype(BF16))
                l_new = jnp.sum(p, axis=0, keepdims=True)
            else:
                m_new = jnp.maximum(m[h], jnp.max(s, axis=0, keepdims=True))
                alpha = jnp.exp(m[h] - m_new)
                p = jnp.exp(s - m_new)
                acc_sc[h] = alpha * acc_sc[h] + _dot(vtj, p.astype(BF16))
                l_new = alpha * l[h] + jnp.sum(p, axis=0, keepdims=True)
            m_out.append(m_new)
            l_out.append(l_new)
        return tuple(m_out), tuple(l_out)

    m, l = lax.cond(i % 2 == 1,
                    lambda: step(i - 1, 2, None, None, True),
                    lambda: step(i, 1, None, None, True))
    m, l = lax.fori_loop(0, i // 2, lambda j, c: step(2 * j, 2, c[0], c[1], False), (m, l))
    for h in range(MLA_HEADS):
        o_ref[0, h] = (acc_sc[h] / l[h]).astype(BF16).T


def _attn_prompt(qcat, kcat, *, tq):
    B, _, T, _ = qcat.shape
    nk = T // tq
    vt = jnp.swapaxes(kcat[:, :, :KV_RANK].reshape(B, nk, tq, KV_RANK), 2, 3)
    return pl.pallas_call(
        functools.partial(_attn_prompt_kernel, tq=tq),
        grid=(B, T // tq),
        in_specs=[pl.BlockSpec((1, MLA_HEADS, tq, KCAT), lambda b, i: (b, 0, i, 0)),
                  pl.BlockSpec((1, T, KCAT), lambda b, i: (b, 0, 0)),
                  pl.BlockSpec((1, nk, KV_RANK, tq), lambda b, i: (b, 0, 0, 0))],
        out_specs=pl.BlockSpec((1, MLA_HEADS, tq, KV_RANK), lambda b, i: (b, 0, i, 0)),
        out_shape=jax.ShapeDtypeStruct((B, MLA_HEADS, T, KV_RANK), BF16),
        scratch_shapes=[pltpu.VMEM((MLA_HEADS, KV_RANK, tq), F32)],
        compiler_params=_cparams(("parallel", "arbitrary")),
        name="mla_attention_prompt",
    )(qcat, kcat, vt)


def _attn_cached_kernel(q_ref, k_ref, o_ref, *, tq, q_pos0, n_keys):
    rows = MLA_HEADS * tq
    q = q_ref[0].reshape(rows, KCAT)
    k = k_ref[0]
    s = _dot_nt(q, k) * MLA_SCALE
    r = lax.broadcasted_iota(jnp.int32, s.shape, 0)
    col = lax.broadcasted_iota(jnp.int32, s.shape, 1)
    q_pos = q_pos0 + (r & (tq - 1))
    visible = ((col // MASK_CHUNK) <= (q_pos // MASK_CHUNK)) & (col < n_keys)
    s = jnp.where(visible, s, NEG_INF)
    p = jnp.exp(s - jnp.max(s, axis=-1, keepdims=True))
    l = jnp.sum(p, axis=-1, keepdims=True)
    o = _dot(p.astype(BF16), k[:, :KV_RANK]) / l
    o_ref[0] = o.astype(BF16).reshape(MLA_HEADS, tq, KV_RANK)


def _attn_cached(qcat, kcat, *, q_pos0, n_keys):
    B, _, tq, _ = qcat.shape
    sp = kcat.shape[1]
    return pl.pallas_call(
        functools.partial(_attn_cached_kernel, tq=tq, q_pos0=q_pos0, n_keys=n_keys),
        grid=(B,),
        in_specs=[pl.BlockSpec((1, MLA_HEADS, tq, KCAT), lambda b: (b, 0, 0, 0)),
                  pl.BlockSpec((1, sp, KCAT), lambda b: (b, 0, 0))],
        out_specs=pl.BlockSpec((1, MLA_HEADS, tq, KV_RANK), lambda b: (b, 0, 0, 0)),
        out_shape=jax.ShapeDtypeStruct((B, MLA_HEADS, tq, KV_RANK), BF16),
        compiler_params=_cparams(("parallel",)),
        name="mla_attention_cached",
    )(qcat, kcat)


def _final_kernel(o_ref, gate_ref, x1_ref, wuv_ref, wout_ref, fnw_ref, y_ref):
    parts = [_dot(o_ref[0, h], wuv_ref[h]) for h in range(MLA_HEADS)]
    o = jnp.concatenate(parts, axis=-1)
    og = (o * gate_ref[0].astype(F32)).astype(BF16)
    x2 = x1_ref[0] + _dot(og, wout_ref[...])
    y_ref[0] = _rms(x2, fnw_ref[...])


def _final(o_lat, gate, x1, wuv, wout, fnw, *, tt):
    B, T, _ = x1.shape
    tok = lambda w: pl.BlockSpec((1, tt, w), lambda b, t: (b, t, 0))
    return pl.pallas_call(
        _final_kernel,
        grid=(B, T // tt),
        in_specs=[pl.BlockSpec((1, MLA_HEADS, tt, KV_RANK), lambda b, t: (b, 0, t, 0)),
                  tok(GATE_W), tok(D_MODEL),
                  _const_spec((MLA_HEADS, KV_RANK, V_HEAD)), _const_spec((GATE_W, D_MODEL)),
                  _const_spec((1, D_MODEL))],
        out_specs=tok(D_MODEL),
        out_shape=jax.ShapeDtypeStruct((B, T, D_MODEL), F32),
        compiler_params=_cparams(("parallel", "parallel")),
        name="mla_final",
    )(o_lat, gate, x1, wuv, wout, fnw)


def _block_diag(w):
    g = BD // QKV_BLOCK
    nb = INNER // BD
    wb = w.reshape(nb, g, QKV_BLOCK, QKV_BLOCK)
    eye = jnp.eye(g, dtype=w.dtype)
    dense = wb[:, :, :, None, :] * eye[None, :, None, :, None]
    return dense.reshape(nb, BD, BD).astype(BF16)


def _pad_lanes(w, width):
    return jnp.pad(w, ((0, 0),) * (w.ndim - 1) + ((0, width - w.shape[-1]),))


def _prep_weights(a_norm_w, a_w_in, a_conv_w, a_conv_b, a_w_q, a_w_k, a_w_v, a_w_gates, a_b_gates, a_outnorm_w,
                  a_skip, a_w_out, kv_norm_w, kv_w_dkv, kv_latent_norm_w, kv_w_uk, kv_w_uv,
                  b_norm_w, b_w_in, b_q_norm_w, b_w_uq, b_w_out, final_norm_w):
    l = 0
    wg = a_w_gates[l]
    wuq = b_w_uq[l].reshape(Q_RANK, MLA_HEADS, QK_NOPE + QK_ROPE)
    wuq_nope = wuq[:, :, :QK_NOPE].reshape(Q_RANK, MLA_HEADS * QK_NOPE)
    wuq_rope = jnp.pad(wuq[:, :, QK_NOPE:], ((0, 0), (0, 0), (0, LANE - QK_ROPE))).reshape(Q_RANK, MLA_HEADS * LANE)
    return dict(
        nw=a_norm_w[l][None], win=a_w_in[l].astype(BF16), cw=a_conv_w[l], cb=a_conv_b[l][None],
        bdq=_block_diag(a_w_q[l]), bdk=_block_diag(a_w_k[l]), bdv=_block_diag(a_w_v[l]),
        wgq=_pad_lanes(wg[:INNER], LANE).astype(BF16), wgk=_pad_lanes(wg[INNER:2 * INNER], LANE).astype(BF16),
        wgv=_pad_lanes(wg[2 * INNER:], LANE).astype(BF16), bg=_pad_lanes(a_b_gates[l][None], LANE),
        onw=a_outnorm_w[l][None], skip=a_skip[l][None], wout_a=a_w_out[l].astype(BF16),
        kvnw=kv_norm_w[None], dkv=_pad_lanes(kv_w_dkv, KCAT).astype(BF16), latw=kv_latent_norm_w[None],
        bnw=b_norm_w[l][None], bwin=b_w_in[l].astype(BF16), qnw=b_q_norm_w[l][None],
        wuq=jnp.concatenate([wuq_nope, wuq_rope], axis=1).astype(BF16),
        wuk=jnp.transpose(kv_w_uk, (1, 2, 0)).astype(BF16),
        wuv=jnp.transpose(kv_w_uv, (1, 0, 2)).astype(BF16),
        wout_b=b_w_out[l].astype(BF16), fnw=final_norm_w[None],
    )


def _rope_tables(pos):
    half = QK_ROPE // 2
    inv = ROPE_BASE ** (-jnp.arange(half, dtype=F32) / half)
    ang = pos.astype(F32)[:, None] * inv[None, :]
    cos, sin = jnp.cos(ang), jnp.sin(ang)
    zero = jnp.zeros((pos.shape[0], LANE - QK_ROPE), F32)
    return jnp.concatenate([cos, cos, zero], axis=1), jnp.concatenate([-sin, sin, zero], axis=1)


def _trunk(x, pos0, state, conv_buf, cache, w, *, tt, chunk):
    B, T, _ = x.shape
    q, k, v, xc, sz, g, conv_tail = _front(x, conv_buf, w["nw"], w["win"], w["cw"], w["cb"], w["bdq"], w["bdk"],
                                           w["bdv"], w["wgq"], w["wgk"], w["wgv"], w["bg"], tt=tt)
    gt = jnp.transpose(g, (0, 2, 1))
    tp = max(T, LANE)
    if tp != T:
        gt = jnp.pad(gt, ((0, 0), (0, 0), (0, tp - T)))
    r = _gate_prep(gt, chunk=chunk)
    rc = jnp.transpose(r[..., :T], (0, 1, 3, 2))
    if state is not None:
        c0, n0, m0 = state
        state = (c0, n0[:, :, None, :], m0[:, :, None, None])
    hn, c_new, n_new, m_new = _recurrence(q, k, v, r, rc, state, chunk=chunk)

    cos_t, sin_t = _rope_tables(pos0 + jnp.arange(T, dtype=jnp.int32))
    x1, ckv, kpe, kcat, qcat, gate = _mid(hn, xc, sz, x, cos_t, sin_t, w["onw"], w["skip"], w["wout_a"], w["kvnw"],
                                          w["dkv"], w["latw"], w["bnw"], w["bwin"], w["qnw"], w["wuq"], w["wuk"],
                                          tt=tt)
    if cache is None:
        o_lat = _attn_prompt(qcat, kcat, tq=tt)
    else:
        past_ckv, past_kpe = cache
        past = past_ckv.shape[1]
        n_keys = past + T
        sp = -(-n_keys // LANE) * LANE
        past_cat = jnp.concatenate([past_ckv.astype(BF16), _pad_lanes(past_kpe, LANE).astype(BF16)], axis=-1)
        keys = jnp.concatenate([past_cat, kcat, jnp.zeros((B, sp - n_keys, KCAT), BF16)], axis=1)
        o_lat = _attn_cached(qcat, keys, q_pos0=pos0, n_keys=n_keys)
    y = _final(o_lat, gate, x1, w["wuv"], w["wout_b"], w["fnw"], tt=tt)
    return (y, c_new[None], n_new[None, :, :, 0, :], m_new[None, :, :, 0, 0], conv_tail[None], ckv, kpe)


def kernel(x_prompt, x_sample, state_mlstm_C, state_mlstm_n, state_mlstm_m, state_mlstm_conv, cache_mla_ckv,
           cache_mla_kpe, a_norm_w, a_w_in, a_conv_w, a_conv_b, a_w_q, a_w_k, a_w_v, a_w_gates, a_b_gates,
           a_outnorm_w, a_skip, a_w_out, kv_norm_w, kv_w_dkv, kv_latent_norm_w, kv_w_uk, kv_w_uv, b_norm_w, b_w_in,
           b_q_norm_w, b_w_uq, b_w_out, final_norm_w):
    w = _prep_weights(a_norm_w, a_w_in, a_conv_w, a_conv_b, a_w_q, a_w_k, a_w_v, a_w_gates, a_b_gates, a_outnorm_w,
                      a_skip, a_w_out, kv_norm_w, kv_w_dkv, kv_latent_norm_w, kv_w_uk, kv_w_uv,
                      b_norm_w, b_w_in, b_q_norm_w, b_w_uq, b_w_out, final_norm_w)
    bp, tp, _ = x_prompt.shape
    bs, ts, _ = x_sample.shape
    past = cache_mla_ckv.shape[1]
    zero_buf = jnp.zeros((bp, CONV_WIDTH - 1, INNER), F32)
    out_p = _trunk(x_prompt, 0, None, zero_buf, None, w, tt=256, chunk=256)
    out_s = _trunk(x_sample, past, (state_mlstm_C[0], state_mlstm_n[0], state_mlstm_m[0]), state_mlstm_conv[0],
                   (cache_mla_ckv, cache_mla_kpe), w, tt=ts, chunk=ts)
    return (out_p[0], out_s[0]) + tuple(out_p[1:]) + tuple(out_s[1:])
```

```python
import functools
import math

import jax
import jax.numpy as jnp
from jax import lax
from jax.experimental import pallas as pl
from jax.experimental.pallas import tpu as pltpu

F32 = jnp.float32
BF16 = jnp.bfloat16

D_MODEL = 1024
INNER = 2 * D_MODEL
N_HEADS = 4
HEAD_DIM = INNER // N_HEADS
QKV_BLOCK = 4
CONV_WIDTH = 4
MLA_HEADS = 8
QK_NOPE = 128
QK_ROPE = 64
V_HEAD = 128
KV_RANK = 256
Q_RANK = 384
GATE_W = MLA_HEADS * V_HEAD
MLA_SCALE = 1.0 / math.sqrt(QK_NOPE + QK_ROPE)
ROPE_BASE = 10000.0
MASK_CHUNK = 64
EPS = 1e-6

LANE = 128
BD = 256
KCAT = KV_RANK + LANE
VMEM_LIMIT = 56 * 1024 * 1024

NEG_INF = float("-inf")


def _cparams(sem):
    return pltpu.CompilerParams(dimension_semantics=sem, vmem_limit_bytes=VMEM_LIMIT)


def _const_spec(shape):
    nd = len(shape)
    return pl.BlockSpec(shape, lambda *_: (0,) * nd, pipeline_mode=pl.Buffered(1))


def _rms(x, w):
    return x * lax.rsqrt(jnp.mean(x * x, axis=-1, keepdims=True) + EPS) * w


def _silu(x):
    return x / (1.0 + jnp.exp(-x))


def _dot(a, b):
    return jnp.dot(a, b, preferred_element_type=F32)


def _dot_nt(a, b):
    return lax.dot_general(a, b, (((1,), (1,)), ((), ())), preferred_element_type=F32)


def _dot_tn(a, b):
    return lax.dot_general(a, b, (((0,), (0,)), ((), ())), preferred_element_type=F32)


def _rope128(x, cos_t, sin_t):
    lane = lax.broadcasted_iota(jnp.int32, x.shape, 1)
    partner = jnp.where(lane < QK_ROPE // 2, pltpu.roll(x, LANE - QK_ROPE // 2, 1), pltpu.roll(x, QK_ROPE // 2, 1))
    return x * cos_t + partner * sin_t


def _front_kernel(x_ref, buf_ref, nw_ref, win_ref, cw_ref, cb_ref, bdq_ref, bdk_ref, bdv_ref,
                  wgc_ref, wgi_ref, bg_ref,
                  q_ref, k_ref, v_ref, xc_ref, sz_ref, g_ref, tail_ref, xs_ref, *, tt, v_transposed):
    t = pl.program_id(1)
    halo = CONV_WIDTH - 1

    @pl.when(t == 0)
    def _():
        xs_ref[8 - halo:8, :] = buf_ref[0]

    xnb = _rms(x_ref[0], nw_ref[...]).astype(BF16)
    x_in = _dot(xnb, win_ref[:, :INNER])
    xs_ref[8:8 + tt, :] = x_in
    sz_ref[0] = _silu(_dot(xnb, win_ref[:, INNER:])).astype(BF16)

    cw = cw_ref[...]
    conv = cb_ref[...] + x_in * cw[halo:halo + 1]
    for j in range(halo):
        conv = conv + xs_ref[8 - halo + j:8 - halo + j + tt, :] * cw[j:j + 1]
    tail_ref[0] = xs_ref[8 + tt - halo:8 + tt, :]
    xs_ref[0:8, :] = xs_ref[tt:tt + 8, :]

    xcb = _silu(conv).astype(BF16)
    xib = x_in.astype(BF16)
    xc_ref[0] = xcb

    g = bg_ref[...] + _dot(xcb, wgc_ref[...]) + _dot(xib, wgi_ref[...])
    g_ref[0] = g[:, :2 * N_HEADS]
    for j in range(INNER // BD):
        sl = slice(j * BD, (j + 1) * BD)
        q_ref[0, :, sl] = _dot(xcb[:, sl], bdq_ref[j]).astype(BF16)
        k_ref[0, :, sl] = (_dot(xcb[:, sl], bdk_ref[j]) * (HEAD_DIM ** -0.5)).astype(BF16)
        if v_transposed:
            v_ref[0, sl, :] = _dot_nt(bdv_ref[j], xib[:, sl]).astype(BF16)
        else:
            v_ref[0, :, sl] = _dot(xib[:, sl], bdv_ref[j]).astype(BF16)


def _front(x, buf, nw, win, cw, cb, bdq, bdk, bdv, wgc, wgi, bg, *, tt, v_transposed):
    B, T, _ = x.shape
    nt = T // tt
    act = jax.ShapeDtypeStruct((B, T, INNER), BF16)
    tok = lambda w: pl.BlockSpec((1, tt, w), lambda b, t: (b, t, 0))
    if v_transposed:
        v_spec = pl.BlockSpec((1, INNER, tt), lambda b, t: (b, 0, t))
        v_shape = jax.ShapeDtypeStruct((B, INNER, T), BF16)
    else:
        v_spec, v_shape = tok(INNER), act
    return pl.pallas_call(
        functools.partial(_front_kernel, tt=tt, v_transposed=v_transposed),
        grid=(B, nt),
        in_specs=[tok(D_MODEL),
                  pl.BlockSpec((1, CONV_WIDTH - 1, INNER), lambda b, t: (b, 0, 0)),
                  _const_spec((1, D_MODEL)), _const_spec((D_MODEL, 2 * INNER)),
                  _const_spec((CONV_WIDTH, INNER)), _const_spec((1, INNER)),
                  _const_spec((INNER // BD, BD, BD)), _const_spec((INNER // BD, BD, BD)),
                  _const_spec((INNER // BD, BD, BD)),
                  _const_spec((INNER, LANE)), _const_spec((INNER, LANE)), _const_spec((1, LANE))],
        out_specs=[tok(INNER), tok(INNER), v_spec, tok(INNER), tok(INNER), tok(2 * N_HEADS),
                   pl.BlockSpec((1, CONV_WIDTH - 1, INNER), lambda b, t: (b, 0, 0))],
        out_shape=[act, act, v_shape, act, act,
                   jax.ShapeDtypeStruct((B, T, 2 * N_HEADS), F32),
                   jax.ShapeDtypeStruct((B, CONV_WIDTH - 1, INNER), F32)],
        scratch_shapes=[pltpu.VMEM((8 + tt, INNER), F32)],
        compiler_params=_cparams(("parallel", "arbitrary")),
        name="mlstm_front",
    )(x, buf, nw, win, cw, cb, bdq, bdk, bdv, wgc, wgi, bg)


def _gate_prep_kernel(gt_ref, r_ref, *, chunk):
    g = gt_ref[0]
    ig = g[:N_HEADS]
    fg = g[N_HEADS:]
    lf = jnp.minimum(fg, 0.0) - jnp.log(1.0 + jnp.exp(-jnp.abs(fg)))
    pos = lax.broadcasted_iota(jnp.int32, lf.shape, 1) & (chunk - 1)
    b = lf
    sh = 1
    while sh < chunk:
        b = b + jnp.where(pos >= sh, pltpu.roll(b, sh, 1), 0.0)
        sh *= 2
    a = ig - b
    cma = a
    sh = 1
    while sh < chunk:
        cma = jnp.maximum(cma, jnp.where(pos >= sh, pltpu.roll(cma, sh, 1), NEG_INF))
        sh *= 2
    zero = jnp.zeros_like(a[0:1])
    for h in range(N_HEADS):
        r_ref[0, h] = jnp.concatenate([a[h:h + 1], b[h:h + 1], cma[h:h + 1]] + [zero] * 5, axis=0)


def _gate_prep(gt, *, chunk):
    B, _, Tp = gt.shape
    return pl.pallas_call(
        functools.partial(_gate_prep_kernel, chunk=chunk),
        grid=(B,),
        in_specs=[pl.BlockSpec((1, 2 * N_HEADS, Tp), lambda b: (b, 0, 0))],
        out_specs=pl.BlockSpec((1, N_HEADS, 8, Tp), lambda b: (b, 0, 0, 0)),
        out_shape=jax.ShapeDtypeStruct((B, N_HEADS, 8, Tp), F32),
        compiler_params=_cparams(("parallel",)),
        name="mlstm_gate_prep",
    )(gt)


def _recur_kernel(*refs, chunk, has_state):
    if has_state:
        q_ref, k_ref, v_ref, r_ref, rc_ref, c0_ref, n0_ref, m0_ref, hn_ref, c_st, n_st, m_st = refs
    else:
        q_ref, k_ref, v_ref, r_ref, rc_ref, hn_ref, c_st, n_st, m_st = refs
    L = chunk
    d = HEAD_DIM

    @pl.when(pl.program_id(1) == 0)
    def _():
        if has_state:
            c_st[...] = c0_ref[...]
            n_st[...] = n0_ref[...]
            m_st[...] = m0_ref[...]
        else:
            c_st[...] = jnp.zeros_like(c_st)
            n_st[...] = jnp.zeros_like(n_st)
            m_st[...] = jnp.zeros_like(m_st)

    ti = lax.broadcasted_iota(jnp.int32, (L, L), 0)
    si = lax.broadcasted_iota(jnp.int32, (L, L), 1)
    causal = si <= ti

    def head_slice(ref, h):
        return ref[0, :, h * d:(h + 1) * d]

    def scores(h):
        q = head_slice(q_ref, h)
        s_raw = _dot_nt(q, head_slice(k_ref, h))
        qc = _dot_nt(q, c_st[0, h].astype(BF16))
        qn = jnp.sum(q.astype(F32) * n_st[0, h], axis=-1, keepdims=True)
        return s_raw, qc, qn, m_st[0, h]

    def update(h, m):
        k = head_slice(k_ref, h)
        a_col = rc_ref[0, h, :, 0:1]
        b_last = rc_ref[0, h, L - 1:L, 1:2]
        a_max = rc_ref[0, h, L - 1:L, 2:3]
        big_n = jnp.maximum(m, a_max)
        decay = jnp.exp(m - big_n)
        w_upd = jnp.exp(a_col - big_n)
        vw = (head_slice(v_ref, h).astype(F32) * w_upd).astype(BF16)
        c_st[0, h] = decay * c_st[0, h] + _dot_tn(vw, k)
        n_st[0, h] = decay * n_st[0, h] + jnp.sum(k.astype(F32) * w_upd, axis=0, keepdims=True)
        m_st[0, h] = b_last + big_n

    def outputs(h, s_raw, qc, qn, m):
        a_row = r_ref[0, h, 0:1, 0:L]
        b_col = rc_ref[0, h, :, 1:2]
        cma_col = rc_ref[0, h, :, 2:3]
        big_m = jnp.maximum(m, cma_col)
        w_intra = jnp.exp(jnp.where(causal, a_row - big_m, NEG_INF))
        w_inter = jnp.exp(m - big_m)
        s = s_raw * w_intra
        num = w_inter * qc + _dot(s.astype(BF16), head_slice(v_ref, h))
        den = w_inter * qn + jnp.sum(s, axis=-1, keepdims=True)
        hh = num / jnp.maximum(jnp.abs(den), jnp.exp(-(b_col + big_m)))
        mu = jnp.mean(hh, axis=-1, keepdims=True)
        hc = hh - mu
        var = jnp.mean(hc * hc, axis=-1, keepdims=True)
        hn_ref[0, :, h * d:(h + 1) * d] = (hc * lax.rsqrt(var + EPS)).astype(BF16)

    pending = {0: scores(0)}
    for h in range(N_HEADS):
        if h + 1 < N_HEADS:
            pending[h + 1] = scores(h + 1)
        s_raw, qc, qn, m = pending.pop(h)
        update(h, m)
        outputs(h, s_raw, qc, qn, m)


def _recurrence(q, k, v, r, rc, state, *, chunk):
    B, T, _ = q.shape
    nc = T // chunk
    d = HEAD_DIM
    has_state = state is not None
    qkv_spec = pl.BlockSpec((1, chunk, INNER), lambda b, c: (b, c, 0))
    lr = r.shape[-1] // nc
    in_specs = [qkv_spec, qkv_spec, qkv_spec,
                pl.BlockSpec((1, N_HEADS, 8, lr), lambda b, c: (b, 0, 0, c)),
                pl.BlockSpec((1, N_HEADS, chunk, 8), lambda b, c: (b, 0, c, 0))]
    args = [q, k, v, r, rc]
    st_specs = [pl.BlockSpec((1, N_HEADS, d, d), lambda b, c: (b, 0, 0, 0)),
                pl.BlockSpec((1, N_HEADS, 1, d), lambda b, c: (b, 0, 0, 0)),
                pl.BlockSpec((1, N_HEADS, 1, 1), lambda b, c: (b, 0, 0, 0))]
    if has_state:
        in_specs += st_specs
        args += list(state)
    return pl.pallas_call(
        functools.partial(_recur_kernel, chunk=chunk, has_state=has_state),
        grid=(B, nc),
        in_specs=in_specs,
        out_specs=[qkv_spec] + st_specs,
        out_shape=[jax.ShapeDtypeStruct((B, T, INNER), BF16),
                   jax.ShapeDtypeStruct((B, N_HEADS, d, d), F32),
                   jax.ShapeDtypeStruct((B, N_HEADS, 1, d), F32),
                   jax.ShapeDtypeStruct((B, N_HEADS, 1, 1), F32)],
        compiler_params=_cparams(("parallel", "arbitrary")),
        name="mlstm_recurrence",
    )(*args)


STATE_ROWS = HEAD_DIM + 16


def _recur_t_kernel(q_ref, k_ref, vt_ref, r_ref, rc_ref, hn_ref, c_out, n_out, m_out, cx_sc, m_sc, *, chunk):
    L = chunk
    d = HEAD_DIM

    @pl.when(pl.program_id(1) == 0)
    def _():
        cx_sc[...] = jnp.zeros_like(cx_sc)
        m_sc[...] = jnp.zeros_like(m_sc)

    si = lax.broadcasted_iota(jnp.int32, (L, L), 0)
    ti = lax.broadcasted_iota(jnp.int32, (L, L), 1)
    causal = si <= ti

    def scores(h):
        q = q_ref[0, :, h * d:(h + 1) * d]
        st_raw = _dot_nt(k_ref[0, :, h * d:(h + 1) * d], q)
        cq = _dot_nt(cx_sc[h].astype(BF16), q)
        return st_raw, cq, m_sc[h]

    def update(h, m):
        a_row = r_ref[0, h, 0:1, :]
        b_last = r_ref[0, h, 1:2, L - 1:L]
        a_max = r_ref[0, h, 2:3, L - 1:L]
        big_n = jnp.maximum(m, a_max)
        decay = jnp.exp(m - big_n)
        w_upd = jnp.exp(a_row - big_n)
        vw = vt_ref[0, h * d:(h + 1) * d, :].astype(F32) * w_upd
        ext = jnp.concatenate([vw, jnp.broadcast_to(w_upd, (STATE_ROWS - d, L))], axis=0).astype(BF16)
        cx_sc[h] = decay * cx_sc[h] + _dot(ext, k_ref[0, :, h * d:(h + 1) * d])
        m_sc[h] = b_last + big_n

    def outputs(h, st_raw, cq, m):
        a_col = rc_ref[0, h, :, 0:1]
        b_row = r_ref[0, h, 1:2, :]
        cma_row = r_ref[0, h, 2:3, :]
        big_m = jnp.maximum(m, cma_row)
        w_intra = jnp.exp(jnp.where(causal, a_col - big_m, NEG_INF))
        w_inter = jnp.exp(m - big_m)
        st = st_raw * w_intra
        num = w_inter * cq[:d] + _dot(vt_ref[0, h * d:(h + 1) * d, :], st.astype(BF16))
        den = w_inter * cq[d:d + 1] + jnp.sum(st, axis=0, keepdims=True)
        hh = num * (1.0 / jnp.maximum(jnp.abs(den), jnp.exp(-(b_row + big_m))))
        mu = jnp.mean(hh, axis=0, keepdims=True)
        hc = hh - mu
        var = jnp.mean(hc * hc, axis=0, keepdims=True)
        hn_ref[0, :, h * d:(h + 1) * d] = (hc * lax.rsqrt(var + EPS)).astype(BF16).T

    pending = {0: scores(0)}
    for h in range(N_HEADS):
        if h + 1 < N_HEADS:
            pending[h + 1] = scores(h + 1)
        st_raw, cq, m = pending.pop(h)
        update(h, m)
        outputs(h, st_raw, cq, m)

    @pl.when(pl.program_id(1) == pl.num_programs(1) - 1)
    def _():
        c_out[0] = cx_sc[:, :d, :]
        n_out[0] = cx_sc[:, d:d + 1, :]
        m_out[0] = m_sc[...]


def _recurrence_t(q, k, vt, r, rc, *, chunk):
    B, T, _ = q.shape
    nc = T // chunk
    d = HEAD_DIM
    qk_spec = pl.BlockSpec((1, chunk, INNER), lambda b, c: (b, c, 0))
    return pl.pallas_call(
        functools.partial(_recur_t_kernel, chunk=chunk),
        grid=(B, nc),
        in_specs=[qk_spec, qk_spec,
                  pl.BlockSpec((1, INNER, chunk), lambda b, c: (b, 0, c)),
                  pl.BlockSpec((1, N_HEADS, 8, chunk), lambda b, c: (b, 0, 0, c)),
                  pl.BlockSpec((1, N_HEADS, chunk, 8), lambda b, c: (b, 0, c, 0))],
        out_specs=[qk_spec,
                   pl.BlockSpec((1, N_HEADS, d, d), lambda b, c: (b, 0, 0, 0)),
                   pl.BlockSpec((1, N_HEADS, 1, d), lambda b, c: (b, 0, 0, 0)),
                   pl.BlockSpec((1, N_HEADS, 1, 1), lambda b, c: (b, 0, 0, 0))],
        out_shape=[jax.ShapeDtypeStruct((B, T, INNER), BF16),
                   jax.ShapeDtypeStruct((B, N_HEADS, d, d), F32),
                   jax.ShapeDtypeStruct((B, N_HEADS, 1, d), F32),
                   jax.ShapeDtypeStruct((B, N_HEADS, 1, 1), F32)],
        scratch_shapes=[pltpu.VMEM((N_HEADS, STATE_ROWS, d), F32), pltpu.VMEM((N_HEADS, 1, 1), F32)],
        compiler_params=_cparams(("parallel", "arbitrary")),
        name="mlstm_recurrence_t",
    )(q, k, vt, r, rc)


def _mid_kernel(hn_ref, xc_ref, sz_ref, x_ref, cos_ref, sin_ref,
                onw_ref, skip_ref, wout_ref, kvnw_ref, dkv_ref, latw_ref,
                bnw_ref, bwin_ref, qnw_ref, wuq_ref, wuk_ref,
                x1_ref, ckv_ref, kpe_ref, kcat_ref, qcat_ref, gate_ref, *maybe_vt_ref):
    cos_t = cos_ref[...]
    sin_t = sin_ref[...]
    y = (hn_ref[0].astype(F32) * onw_ref[...] + skip_ref[...] * xc_ref[0].astype(F32)) * sz_ref[0].astype(F32)
    x1 = x_ref[0] + _dot(y.astype(BF16), wout_ref[...])
    x1_ref[0] = x1

    ukv = _dot(_rms(x1, kvnw_ref[...]).astype(BF16), dkv_ref[...])
    ckv = _rms(ukv[:, :KV_RANK], latw_ref[...])
    kpe = _rope128(ukv[:, KV_RANK:], cos_t, sin_t)
    ckv_ref[0] = ckv
    kpe_ref[0] = kpe[:, :QK_ROPE]
    kcat_ref[0, :, :KV_RANK] = ckv.astype(BF16)
    kcat_ref[0, :, KV_RANK:] = kpe.astype(BF16)
    for vt_ref in maybe_vt_ref:
        vt_ref[0, 0] = ckv.astype(BF16).T

    ub =_dot(_rms(x1, bnw_ref[...]).astype(BF16), bwin_ref[...])
    gate_ref[0] = _silu(ub[:, Q_RANK:]).astype(BF16)
    cq = _rms(ub[:, :Q_RANK], qnw_ref[...]).astype(BF16)
    qq = _dot(cq, wuq_ref[...])
    nope_w = MLA_HEADS * QK_NOPE
    for h in range(MLA_HEADS):
        q_lat = _dot(qq[:, h * QK_NOPE:(h + 1) * QK_NOPE].astype(BF16), wuk_ref[h])
        q_pe = _rope128(qq[:, nope_w + h * LANE:nope_w + (h + 1) * LANE], cos_t, sin_t)
        qcat_ref[0, h, :, :KV_RANK] = q_lat.astype(BF16)
        qcat_ref[0, h, :, KV_RANK:] = q_pe.astype(BF16)


def _mid(hn, xc, sz, x, cos_t, sin_t, onw, skip, wout, kvnw, dkv, latw, bnw, bwin, qnw, wuq, wuk, *, tt, emit_vt):
    B, T, _ = x.shape
    nt = T // tt
    tok = lambda w: pl.BlockSpec((1, tt, w), lambda b, t: (b, t, 0))
    pos = pl.BlockSpec((tt, LANE), lambda b, t: (t, 0))
    vt_specs = [pl.BlockSpec((1, 1, KV_RANK, tt), lambda b, t: (b, t, 0, 0))] if emit_vt else []
    vt_shapes = [jax.ShapeDtypeStruct((B, nt, KV_RANK, tt), BF16)] if emit_vt else []
    return pl.pallas_call(
        _mid_kernel,
        grid=(B, nt),
        in_specs=[tok(INNER), tok(INNER), tok(INNER), tok(D_MODEL), pos, pos,
                  _const_spec((1, INNER)), _const_spec((1, INNER)), _const_spec((INNER, D_MODEL)),
                  _const_spec((1, D_MODEL)), _const_spec((D_MODEL, KCAT)), _const_spec((1, KV_RANK)),
                  _const_spec((1, D_MODEL)), _const_spec((D_MODEL, Q_RANK + GATE_W)), _const_spec((1, Q_RANK)),
                  _const_spec((Q_RANK, 2 * MLA_HEADS * LANE)), _const_spec((MLA_HEADS, QK_NOPE, KV_RANK))],
        out_specs=[tok(D_MODEL), tok(KV_RANK), tok(QK_ROPE), tok(KCAT),
                   pl.BlockSpec((1, MLA_HEADS, tt, KCAT), lambda b, t: (b, 0, t, 0)),
                   tok(GATE_W)] + vt_specs,
        out_shape=[jax.ShapeDtypeStruct((B, T, D_MODEL), F32),
                   jax.ShapeDtypeStruct((B, T, KV_RANK), F32),
                   jax.ShapeDtypeStruct((B, T, QK_ROPE), F32),
                   jax.ShapeDtypeStruct((B, T, KCAT), BF16),
                   jax.ShapeDtypeStruct((B, MLA_HEADS, T, KCAT), BF16),
                   jax.ShapeDtypeStruct((B, T, GATE_W), BF16)] + vt_shapes,
        compiler_params=_cparams(("parallel", "parallel")),
        name="mid_proj",
    )(hn, xc, sz, x, cos_t, sin_t, onw, skip, wout, kvnw, dkv, latw, bnw, bwin, qnw, wuq, wuk)


def _attn_prompt_kernel(q_ref, k_ref, vt_ref, o_ref, acc_sc, *, tq):
    i = pl.program_id(1)

    def step(b0, nb, m, l, first):
        tk = nb * tq
        kj = k_ref[0, pl.ds(pl.multiple_of(b0 * tq, tq), tk), :]
        vtj = jnp.concatenate([vt_ref[0, b0 + n] for n in range(nb)], axis=1) if nb > 1 else vt_ref[0, b0]
        if first:
            ks = lax.broadcasted_iota(jnp.int32, (tk, tq), 0) - (nb - 1) * tq
            qs = lax.broadcasted_iota(jnp.int32, (tk, tq), 1)
            visible = (ks < 0) | ((ks // MASK_CHUNK) <= (qs // MASK_CHUNK))
        m_out, l_out = [], []
        ahead = 4
        raw = [_dot_nt(kj, q_ref[0, h]) for h in range(ahead)]
        for h in range(MLA_HEADS):
            if h + ahead < MLA_HEADS:
                raw.append(_dot_nt(kj, q_ref[0, h + ahead]))
            s = raw[h] * MLA_SCALE
            if first:
                s = jnp.where(visible, s, NEG_INF)
                m_new = jnp.max(s, axis=0, keepdims=True)
                p = jnp.exp(s - m_new)
                acc_sc[h] = _dot(vtj, p.astype(BF16))
                l_new = jnp.sum(p, axis=0, keepdims=True)
            else:
                m_new = jnp.maximum(m[h], jnp.max(s, axis=0, keepdims=True))
                alpha = jnp.exp(m[h] - m_new)
                p = jnp.exp(s - m_new)
                acc_sc[h] = alpha * acc_sc[h] + _dot(vtj, p.astype(BF16))
                l_new = alpha * l[h] + jnp.sum(p, axis=0, keepdims=True)
            m_out.append(m_new)
            l_out.append(l_new)
        return tuple(m_out), tuple(l_out)

    m, l = lax.cond(i % 2 == 1,
                    lambda: step(i - 1, 2, None, None, True),
                    lambda: step(i, 1, None, None, True))
    m, l = lax.fori_loop(0, i // 2, lambda j, c: step(2 * j, 2, c[0], c[1], False), (m, l))
    for h in range(MLA_HEADS):
        o_ref[0, h] = (acc_sc[h] / l[h]).astype(BF16).T


def _attn_prompt(qcat, kcat, vt, *, tq):
    B, _, T, _ = qcat.shape
    nk = T // tq
    return pl.pallas_call(
        functools.partial(_attn_prompt_kernel, tq=tq),
        grid=(B, T // tq),
        in_specs=[pl.BlockSpec((1, MLA_HEADS, tq, KCAT), lambda b, i: (b, 0, i, 0)),
                  pl.BlockSpec((1, T, KCAT), lambda b, i: (b, 0, 0)),
                  pl.BlockSpec((1, nk, KV_RANK, tq), lambda b, i: (b, 0, 0, 0))],
        out_specs=pl.BlockSpec((1, MLA_HEADS, tq, KV_RANK), lambda b, i: (b, 0, i, 0)),
        out_shape=jax.ShapeDtypeStruct((B, MLA_HEADS, T, KV_RANK), BF16),
        scratch_shapes=[pltpu.VMEM((MLA_HEADS, KV_RANK, tq), F32)],
        compiler_params=_cparams(("parallel", "arbitrary")),
        name="mla_attention_prompt",
    )(qcat, kcat, vt)


def _attn_cached_kernel(q_ref, pckv_ref, pkpe_ref, knew_ref, o_ref, *, tq, past):
    rows = MLA_HEADS * tq
    q = q_ref[0].reshape(rows, KCAT)
    ckv = pckv_ref[0].astype(BF16)
    kpe = pkpe_ref[0].astype(BF16)
    knew = knew_ref[0]
    s_old = (_dot_nt(q[:, :KV_RANK], ckv) + _dot_nt(q[:, KV_RANK:KV_RANK + QK_ROPE], kpe)) * MLA_SCALE
    s_new = _dot_nt(q, knew) * MLA_SCALE

    def masked(s, key_pos0):
        r = lax.broadcasted_iota(jnp.int32, s.shape, 0)
        key_pos = key_pos0 + lax.broadcasted_iota(jnp.int32, s.shape, 1)
        q_pos = past + (r & (tq - 1))
        return jnp.where((key_pos // MASK_CHUNK) <= (q_pos // MASK_CHUNK), s, NEG_INF)

    s_old = masked(s_old, 0)
    s_new = masked(s_new, past)
    m = jnp.maximum(jnp.max(s_old, axis=-1, keepdims=True), jnp.max(s_new, axis=-1, keepdims=True))
    p_old = jnp.exp(s_old - m)
    p_new = jnp.exp(s_new - m)
    l = jnp.sum(p_old, axis=-1, keepdims=True) + jnp.sum(p_new, axis=-1, keepdims=True)
    o = (_dot(p_old.astype(BF16), ckv) + _dot(p_new.astype(BF16), knew[:, :KV_RANK])) / l
    o_ref[0] = o.astype(BF16).reshape(MLA_HEADS, tq, KV_RANK)


def _attn_cached(qcat, past_ckv, past_kpe, kcat):
    B, _, tq, _ = qcat.shape
    past = past_ckv.shape[1]
    return pl.pallas_call(
        functools.partial(_attn_cached_kernel, tq=tq, past=past),
        grid=(B,),
        in_specs=[pl.BlockSpec((1, MLA_HEADS, tq, KCAT), lambda b: (b, 0, 0, 0)),
                  pl.BlockSpec((1, past, KV_RANK), lambda b: (b, 0, 0)),
                  pl.BlockSpec((1, past, QK_ROPE), lambda b: (b, 0, 0)),
                  pl.BlockSpec((1, tq, KCAT), lambda b: (b, 0, 0))],
        out_specs=pl.BlockSpec((1, MLA_HEADS, tq, KV_RANK), lambda b: (b, 0, 0, 0)),
        out_shape=jax.ShapeDtypeStruct((B, MLA_HEADS, tq, KV_RANK), BF16),
        compiler_params=_cparams(("parallel",)),
        name="mla_attention_cached",
    )(qcat, past_ckv, past_kpe, kcat)


def _final_kernel(o_ref, gate_ref, x1_ref, wuv_ref, wout_ref, fnw_ref, y_ref):
    parts = [_dot(o_ref[0, h], wuv_ref[h]) for h in range(MLA_HEADS)]
    o = jnp.concatenate(parts, axis=-1)
    og = (o * gate_ref[0].astype(F32)).astype(BF16)
    x2 = x1_ref[0] + _dot(og, wout_ref[...])
    y_ref[0] = _rms(x2, fnw_ref[...])


def _final(o_lat, gate, x1, wuv, wout, fnw, *, tt):
    B, T, _ = x1.shape
    tok = lambda w: pl.BlockSpec((1, tt, w), lambda b, t: (b, t, 0))
    return pl.pallas_call(
        _final_kernel,
        grid=(B, T // tt),
        in_specs=[pl.BlockSpec((1, MLA_HEADS, tt, KV_RANK), lambda b, t: (b, 0, t, 0)),
                  tok(GATE_W), tok(D_MODEL),
                  _const_spec((MLA_HEADS, KV_RANK, V_HEAD)), _const_spec((GATE_W, D_MODEL)),
                  _const_spec((1, D_MODEL))],
        out_specs=tok(D_MODEL),
        out_shape=jax.ShapeDtypeStruct((B, T, D_MODEL), F32),
        compiler_params=_cparams(("parallel", "parallel")),
        name="mla_final",
    )(o_lat, gate, x1, wuv, wout, fnw)


def _block_diag(w):
    g = BD // QKV_BLOCK
    nb = INNER // BD
    wb = w.reshape(nb, g, QKV_BLOCK, QKV_BLOCK)
    eye = jnp.eye(g, dtype=w.dtype)
    dense = wb[:, :, :, None, :] * eye[None, :, None, :, None]
    return dense.reshape(nb, BD, BD).astype(BF16)


def _fold_gates(w_head, w_gate):
    wg = w_gate.reshape(INNER // QKV_BLOCK, QKV_BLOCK, 2 * N_HEADS)
    return jnp.einsum("gio,gon->gin", w_head, wg, precision=lax.Precision.HIGHEST).reshape(INNER, 2 * N_HEADS)


def _pad_lanes(w, width):
    return jnp.pad(w, ((0, 0),) * (w.ndim - 1) + ((0, width - w.shape[-1]),))


def _prep_weights(a_norm_w, a_w_in, a_conv_w, a_conv_b, a_w_q, a_w_k, a_w_v, a_w_gates, a_b_gates, a_outnorm_w,
                  a_skip, a_w_out, kv_norm_w, kv_w_dkv, kv_latent_norm_w, kv_w_uk, kv_w_uv,
                  b_norm_w, b_w_in, b_q_norm_w, b_w_uq, b_w_out, final_norm_w):
    l = 0
    wg = a_w_gates[l]
    wuq = b_w_uq[l].reshape(Q_RANK, MLA_HEADS, QK_NOPE + QK_ROPE)
    wuq_nope = wuq[:, :, :QK_NOPE].reshape(Q_RANK, MLA_HEADS * QK_NOPE)
    wuq_rope = jnp.pad(wuq[:, :, QK_NOPE:], ((0, 0), (0, 0), (0, LANE - QK_ROPE))).reshape(Q_RANK, MLA_HEADS * LANE)
    return dict(
        nw=a_norm_w[l][None], win=a_w_in[l].astype(BF16), cw=a_conv_w[l], cb=a_conv_b[l][None],
        bdq=_block_diag(a_w_q[l]), bdk=_block_diag(a_w_k[l]), bdv=_block_diag(a_w_v[l]),
        bdv_t=jnp.swapaxes(_block_diag(a_w_v[l]), 1, 2),
        wgc=_pad_lanes(_fold_gates(a_w_q[l], wg[:INNER]) + _fold_gates(a_w_k[l], wg[INNER:2 * INNER]),
                       LANE).astype(BF16),
        wgi=_pad_lanes(_fold_gates(a_w_v[l], wg[2 * INNER:]), LANE).astype(BF16),
        bg=_pad_lanes(a_b_gates[l][None], LANE),
        onw=a_outnorm_w[l][None], skip=a_skip[l][None], wout_a=a_w_out[l].astype(BF16),
        kvnw=kv_norm_w[None], dkv=_pad_lanes(kv_w_dkv, KCAT).astype(BF16), latw=kv_latent_norm_w[None],
        bnw=b_norm_w[l][None], bwin=b_w_in[l].astype(BF16), qnw=b_q_norm_w[l][None],
        wuq=jnp.concatenate([wuq_nope, wuq_rope], axis=1).astype(BF16),
        wuk=jnp.transpose(kv_w_uk, (1, 2, 0)).astype(BF16),
        wuv=jnp.transpose(kv_w_uv, (1, 0, 2)).astype(BF16),
        wout_b=b_w_out[l].astype(BF16), fnw=final_norm_w[None],
    )


def _rope_tables(pos):
    half = QK_ROPE // 2
    inv = ROPE_BASE ** (-jnp.arange(half, dtype=F32) / half)
    ang = pos.astype(F32)[:, None] * inv[None, :]
    cos, sin = jnp.cos(ang), jnp.sin(ang)
    zero = jnp.zeros((pos.shape[0], LANE - QK_ROPE), F32)
    return jnp.concatenate([cos, cos, zero], axis=1), jnp.concatenate([-sin, sin, zero], axis=1)


def _trunk(x, pos0, state, conv_buf, cache, w, *, tt, chunk):
    B, T, _ = x.shape
    time_on_lanes = state is None and T % LANE == 0
    q, k, v, xc, sz, g, conv_tail = _front(x, conv_buf, w["nw"], w["win"], w["cw"], w["cb"], w["bdq"], w["bdk"],
                                           w["bdv_t"] if time_on_lanes else w["bdv"], w["wgc"], w["wgi"], w["bg"],
                                           tt=tt, v_transposed=time_on_lanes)
    gt = jnp.transpose(g, (0, 2, 1))
    tp = max(T, LANE)
    if tp != T:
        gt = jnp.pad(gt, ((0, 0), (0, 0), (0, tp - T)))
    r = _gate_prep(gt, chunk=chunk)
    rc = jnp.transpose(r[..., :T], (0, 1, 3, 2))
    if time_on_lanes:
        hn, c_new, n_new, m_new = _recurrence_t(q, k, v, r, rc, chunk=chunk)
    else:
        if state is not None:
            c0, n0, m0 = state
            state = (c0, n0[:, :, None, :], m0[:, :, None, None])
        hn, c_new, n_new, m_new = _recurrence(q, k, v, r, rc, state, chunk=chunk)

    cos_t, sin_t = _rope_tables(pos0 + jnp.arange(T, dtype=jnp.int32))
    x1, ckv, kpe, kcat, qcat, gate, *vt = _mid(hn, xc, sz, x, cos_t, sin_t, w["onw"], w["skip"], w["wout_a"],
                                               w["kvnw"], w["dkv"], w["latw"], w["bnw"], w["bwin"], w["qnw"],
                                               w["wuq"], w["wuk"], tt=tt, emit_vt=cache is None)
    if cache is None:
        o_lat = _attn_prompt(qcat, kcat, vt[0], tq=tt)
    else:
        past_ckv, past_kpe = cache
        assert pos0 == past_ckv.shape[1]
        o_lat = _attn_cached(qcat, past_ckv, past_kpe, kcat)
    y = _final(o_lat, gate, x1, w["wuv"], w["wout_b"], w["fnw"], tt=2 * tt if T % (2 * tt) == 0 else tt)
    return (y, c_new[None], n_new[None, :, :, 0, :], m_new[None, :, :, 0, 0], conv_tail[None], ckv, kpe)


def kernel(x_prompt, x_sample, state_mlstm_C, state_mlstm_n, state_mlstm_m, state_mlstm_conv, cache_mla_ckv,
           cache_mla_kpe, a_norm_w, a_w_in, a_conv_w, a_conv_b, a_w_q, a_w_k, a_w_v, a_w_gates, a_b_gates,
           a_outnorm_w, a_skip, a_w_out, kv_norm_w, kv_w_dkv, kv_latent_norm_w, kv_w_uk, kv_w_uv, b_norm_w, b_w_in,
           b_q_norm_w, b_w_uq, b_w_out, final_norm_w):
    w = _prep_weights(a_norm_w, a_w_in, a_conv_w, a_conv_b, a_w_q, a_w_k, a_w_v, a_w_gates, a_b_gates, a_outnorm_w,
                      a_skip, a_w_out, kv_norm_w, kv_w_dkv, kv_latent_norm_w, kv_w_uk, kv_w_uv,
                      b_norm_w, b_w_in, b_q_norm_w, b_w_uq, b_w_out, final_norm_w)
    bp, tp, _ = x_prompt.shape
    bs, ts, _ = x_sample.shape
    past = cache_mla_ckv.shape[1]
    zero_buf = jnp.zeros((bp, CONV_WIDTH - 1, INNER), F32)
    out_p = _trunk(x_prompt, 0, None, zero_buf, None, w, tt=256, chunk=256)
    out_s = _trunk(x_sample, past, (state_mlstm_C[0], state_mlstm_n[0], state_mlstm_m[0]), state_mlstm_conv[0],
                   (cache_mla_ckv, cache_mla_kpe), w, tt=ts, chunk=ts)
    return (out_p[0], out_s[0]) + tuple(out_p[1:]) + tuple(out_s[1:])
```

```python
import functools
import math

import jax
import jax.numpy as jnp
from jax import lax
from jax.experimental import pallas as pl
from jax.experimental.pallas import tpu as pltpu

F32 = jnp.float32
BF16 = jnp.bfloat16

D_MODEL = 1024
INNER = 2 * D_MODEL
N_HEADS = 4
HEAD_DIM = INNER // N_HEADS
QKV_BLOCK = 4
CONV_WIDTH = 4
MLA_HEADS = 8
QK_NOPE = 128
QK_ROPE = 64
V_HEAD = 128
KV_RANK = 256
Q_RANK = 384
GATE_W = MLA_HEADS * V_HEAD
MLA_SCALE = 1.0 / math.sqrt(QK_NOPE + QK_ROPE)
ROPE_BASE = 10000.0
MASK_CHUNK = 64
EPS = 1e-6

LANE = 128
GATE_ROWS = 16
BD = 256
KCAT = KV_RANK + LANE
VMEM_LIMIT = 56 * 1024 * 1024

NEG_INF = float("-inf")


def _cparams(sem):
    return pltpu.CompilerParams(dimension_semantics=sem, vmem_limit_bytes=VMEM_LIMIT)


def _const_spec(shape):
    nd = len(shape)
    return pl.BlockSpec(shape, lambda *_: (0,) * nd, pipeline_mode=pl.Buffered(1))


def _rms(x, w):
    return x * lax.rsqrt(jnp.mean(x * x, axis=-1, keepdims=True) + EPS) * w


def _silu(x):
    return x / (1.0 + jnp.exp(-x))


def _dot(a, b):
    return jnp.dot(a, b, preferred_element_type=F32)


def _dot_nt(a, b):
    return lax.dot_general(a, b, (((1,), (1,)), ((), ())), preferred_element_type=F32)


def _dot_tn(a, b):
    return lax.dot_general(a, b, (((0,), (0,)), ((), ())), preferred_element_type=F32)


def _rope128(x, cos_t, sin_t):
    lane = lax.broadcasted_iota(jnp.int32, x.shape, 1)
    partner = jnp.where(lane < QK_ROPE // 2, pltpu.roll(x, LANE - QK_ROPE // 2, 1), pltpu.roll(x, QK_ROPE // 2, 1))
    return x * cos_t + partner * sin_t


def _front_kernel(x_ref, buf_ref, nw_ref, win_ref, cw_ref, cb_ref, bdq_ref, bdk_ref, bdv_ref,
                  wgc_ref, wgi_ref, bg_ref,
                  q_ref, k_ref, v_ref, xc_ref, sz_ref, g_ref, tail_ref, *rest, tt, v_transposed):
    xs_ref = rest[-1]
    t = pl.program_id(1)
    halo = CONV_WIDTH - 1

    @pl.when(t == 0)
    def _():
        xs_ref[8 - halo:8, :] = buf_ref[0]

    xnb = _rms(x_ref[0], nw_ref[...]).astype(BF16)
    x_in = _dot(xnb, win_ref[:, :INNER])
    xs_ref[8:8 + tt, :] = x_in

    cw = cw_ref[...]
    conv = cb_ref[...] + x_in * cw[halo:halo + 1]
    for j in range(halo):
        conv = conv + xs_ref[8 - halo + j:8 - halo + j + tt, :] * cw[j:j + 1]
    tail_ref[0] = xs_ref[8 + tt - halo:8 + tt, :]
    xs_ref[0:8, :] = xs_ref[tt:tt + 8, :]

    xcb = _silu(conv).astype(BF16)
    xib = x_in.astype(BF16)
    xc_ref[0] = xcb

    if v_transposed:
        g = bg_ref[...] + _dot_nt(wgc_ref[...], xcb) + _dot_nt(wgi_ref[...], xib)
        a, b, cma = _gate_rows(g[:2 * N_HEADS], tt)
        g_ref[0, 0:N_HEADS] = a
        g_ref[0, N_HEADS:2 * N_HEADS] = b
        g_ref[0, 2 * N_HEADS:3 * N_HEADS] = cma
    else:
        g = bg_ref[...] + _dot(xcb, wgc_ref[...]) + _dot(xib, wgi_ref[...])
        g_ref[0] = g[:, :2 * N_HEADS]
    sz_ref[0] = _silu(_dot(xnb, win_ref[:, INNER:])).astype(BF16)
    for j in range(INNER // BD):
        sl = slice(j * BD, (j + 1) * BD)
        q_ref[0, :, sl] = _dot(xcb[:, sl], bdq_ref[j]).astype(BF16)
        k_ref[0, :, sl] = (_dot(xcb[:, sl], bdk_ref[j]) * (HEAD_DIM ** -0.5)).astype(BF16)
        if v_transposed:
            v_ref[0, sl, :] = _dot_nt(bdv_ref[j], xib[:, sl]).astype(BF16)
        else:
            v_ref[0, :, sl] = _dot(xib[:, sl], bdv_ref[j]).astype(BF16)
    if v_transposed:
        rest[0][0] = _rows_to_cols(jnp.concatenate([a, a], axis=0))


def _front(x, buf, nw, win, cw, cb, bdq, bdk, bdv, wgc, wgi, bg, *, tt, v_transposed):
    B, T, _ = x.shape
    nt = T // tt
    act = jax.ShapeDtypeStruct((B, T, INNER), BF16)
    tok = lambda w: pl.BlockSpec((1, tt, w), lambda b, t: (b, t, 0))
    if v_transposed:
        v_spec = pl.BlockSpec((1, INNER, tt), lambda b, t: (b, 0, t))
        v_shape = jax.ShapeDtypeStruct((B, INNER, T), BF16)
        g_spec = pl.BlockSpec((1, 3 * N_HEADS, tt), lambda b, t: (b, 0, t))
        g_shape = jax.ShapeDtypeStruct((B, 3 * N_HEADS, T), F32)
        gate_w_specs = [_const_spec((GATE_ROWS, INNER)), _const_spec((GATE_ROWS, INNER)), _const_spec((GATE_ROWS, 1))]
        extra_specs, extra_shapes = [tok(2 * N_HEADS)], [jax.ShapeDtypeStruct((B, T, 2 * N_HEADS), F32)]
    else:
        extra_specs, extra_shapes = [], []
        v_spec, v_shape = tok(INNER), act
        g_spec, g_shape = tok(2 * N_HEADS), jax.ShapeDtypeStruct((B, T, 2 * N_HEADS), F32)
        gate_w_specs = [_const_spec((INNER, LANE)), _const_spec((INNER, LANE)), _const_spec((1, LANE))]
    return pl.pallas_call(
        functools.partial(_front_kernel, tt=tt, v_transposed=v_transposed),
        grid=(B, nt),
        in_specs=[tok(D_MODEL),
                  pl.BlockSpec((1, CONV_WIDTH - 1, INNER), lambda b, t: (b, 0, 0)),
                  _const_spec((1, D_MODEL)), _const_spec((D_MODEL, 2 * INNER)),
                  _const_spec((CONV_WIDTH, INNER)), _const_spec((1, INNER)),
                  _const_spec((INNER // BD, BD, BD)), _const_spec((INNER // BD, BD, BD)),
                  _const_spec((INNER // BD, BD, BD)),
                  ] + gate_w_specs,
        out_specs=[tok(INNER), tok(INNER), v_spec, tok(INNER), tok(INNER), g_spec,
                   pl.BlockSpec((1, CONV_WIDTH - 1, INNER), lambda b, t: (b, 0, 0))] + extra_specs,
        out_shape=[act, act, v_shape, act, act, g_shape,
                   jax.ShapeDtypeStruct((B, CONV_WIDTH - 1, INNER), F32)] + extra_shapes,
        scratch_shapes=[pltpu.VMEM((8 + tt, INNER), F32)],
        compiler_params=_cparams(("parallel", "arbitrary")),
        name="mlstm_front",
    )(x, buf, nw, win, cw, cb, bdq, bdk, bdv, wgc, wgi, bg)


def _gate_rows(g, chunk):
    ig = g[:N_HEADS]
    fg = g[N_HEADS:]
    lf = jnp.minimum(fg, 0.0) - jnp.log(1.0 + jnp.exp(-jnp.abs(fg)))
    pos = lax.broadcasted_iota(jnp.int32, lf.shape, 1) & (chunk - 1)
    b = lf
    sh = 1
    while sh < chunk:
        b = b + jnp.where(pos >= sh, pltpu.roll(b, sh, 1), 0.0)
        sh *= 2
    a = ig - b
    cma = a
    sh = 1
    while sh < chunk:
        cma = jnp.maximum(cma, jnp.where(pos >= sh, pltpu.roll(cma, sh, 1), NEG_INF))
        sh *= 2
    return a, b, cma


def _rows_to_cols(rows):
    n = rows.shape[1]
    eye = (lax.broadcasted_iota(jnp.int32, (n, n), 0) == lax.broadcasted_iota(jnp.int32, (n, n), 1))
    eye = jnp.where(eye, 1.0, 0.0).astype(BF16)
    p1 = rows.astype(BF16)
    r1 = rows - p1.astype(F32)
    p2 = r1.astype(BF16)
    p3 = (r1 - p2.astype(F32)).astype(BF16)
    return (_dot_nt(eye, p1) + _dot_nt(eye, p2)) + _dot_nt(eye, p3)


def _gate_prep_kernel(gt_ref, r_ref, *, chunk):
    a, b, cma = _gate_rows(gt_ref[0], chunk)
    zero = jnp.zeros_like(a[0:1])
    for h in range(N_HEADS):
        r_ref[0, h] = jnp.concatenate([a[h:h + 1], b[h:h + 1], cma[h:h + 1]] + [zero] * 5, axis=0)


def _gate_prep(gt, *, chunk):
    B, _, Tp = gt.shape
    return pl.pallas_call(
        functools.partial(_gate_prep_kernel, chunk=chunk),
        grid=(B,),
        in_specs=[pl.BlockSpec((1, 2 * N_HEADS, Tp), lambda b: (b, 0, 0))],
        out_specs=pl.BlockSpec((1, N_HEADS, 8, Tp), lambda b: (b, 0, 0, 0)),
        out_shape=jax.ShapeDtypeStruct((B, N_HEADS, 8, Tp), F32),
        compiler_params=_cparams(("parallel",)),
        name="mlstm_gate_prep",
    )(gt)


def _recur_kernel(*refs, chunk, has_state):
    if has_state:
        q_ref, k_ref, v_ref, r_ref, rc_ref, c0_ref, n0_ref, m0_ref, hn_ref, c_st, n_st, m_st = refs
    else:
        q_ref, k_ref, v_ref, r_ref, rc_ref, hn_ref, c_st, n_st, m_st = refs
    L = chunk
    d = HEAD_DIM

    @pl.when(pl.program_id(1) == 0)
    def _():
        if has_state:
            c_st[...] = c0_ref[...]
            n_st[...] = n0_ref[...]
            m_st[...] = m0_ref[...]
        else:
            c_st[...] = jnp.zeros_like(c_st)
            n_st[...] = jnp.zeros_like(n_st)
            m_st[...] = jnp.zeros_like(m_st)

    ti = lax.broadcasted_iota(jnp.int32, (L, L), 0)
    si = lax.broadcasted_iota(jnp.int32, (L, L), 1)
    causal = si <= ti

    def head_slice(ref, h):
        return ref[0, :, h * d:(h + 1) * d]

    def scores(h):
        q = head_slice(q_ref, h)
        s_raw = _dot_nt(q, head_slice(k_ref, h))
        qc = _dot_nt(q, c_st[0, h].astype(BF16))
        qn = jnp.sum(q.astype(F32) * n_st[0, h], axis=-1, keepdims=True)
        return s_raw, qc, qn, m_st[0, h]

    def update(h, m):
        k = head_slice(k_ref, h)
        a_col = rc_ref[0, h, :, 0:1]
        b_last = rc_ref[0, h, L - 1:L, 1:2]
        a_max = rc_ref[0, h, L - 1:L, 2:3]
        big_n = jnp.maximum(m, a_max)
        decay = jnp.exp(m - big_n)
        w_upd = jnp.exp(a_col - big_n)
        vw = (head_slice(v_ref, h).astype(F32) * w_upd).astype(BF16)
        c_st[0, h] = decay * c_st[0, h] + _dot_tn(vw, k)
        n_st[0, h] = decay * n_st[0, h] + jnp.sum(k.astype(F32) * w_upd, axis=0, keepdims=True)
        m_st[0, h] = b_last + big_n

    def outputs(h, s_raw, qc, qn, m):
        a_row = r_ref[0, h, 0:1, 0:L]
        b_col = rc_ref[0, h, :, 1:2]
        cma_col = rc_ref[0, h, :, 2:3]
        big_m = jnp.maximum(m, cma_col)
        w_intra = jnp.exp(jnp.where(causal, a_row - big_m, NEG_INF))
        w_inter = jnp.exp(m - big_m)
        s = s_raw * w_intra
        num = w_inter * qc + _dot(s.astype(BF16), head_slice(v_ref, h))
        den = w_inter * qn + jnp.sum(s, axis=-1, keepdims=True)
        hh = num / jnp.maximum(jnp.abs(den), jnp.exp(-(b_col + big_m)))
        mu = jnp.mean(hh, axis=-1, keepdims=True)
        hc = hh - mu
        var = jnp.mean(hc * hc, axis=-1, keepdims=True)
        hn_ref[0, :, h * d:(h + 1) * d] = (hc * lax.rsqrt(var + EPS)).astype(BF16)

    pending = {0: scores(0)}
    for h in range(N_HEADS):
        if h + 1 < N_HEADS:
            pending[h + 1] = scores(h + 1)
        s_raw, qc, qn, m = pending.pop(h)
        update(h, m)
        outputs(h, s_raw, qc, qn, m)


def _recurrence(q, k, v, r, rc, state, *, chunk):
    B, T, _ = q.shape
    nc = T // chunk
    d = HEAD_DIM
    has_state = state is not None
    qkv_spec = pl.BlockSpec((1, chunk, INNER), lambda b, c: (b, c, 0))
    lr = r.shape[-1] // nc
    in_specs = [qkv_spec, qkv_spec, qkv_spec,
                pl.BlockSpec((1, N_HEADS, 8, lr), lambda b, c: (b, 0, 0, c)),
                pl.BlockSpec((1, N_HEADS, chunk, 8), lambda b, c: (b, 0, c, 0))]
    args = [q, k, v, r, rc]
    st_specs = [pl.BlockSpec((1, N_HEADS, d, d), lambda b, c: (b, 0, 0, 0)),
                pl.BlockSpec((1, N_HEADS, 1, d), lambda b, c: (b, 0, 0, 0)),
                pl.BlockSpec((1, N_HEADS, 1, 1), lambda b, c: (b, 0, 0, 0))]
    if has_state:
        in_specs += st_specs
        args += list(state)
    return pl.pallas_call(
        functools.partial(_recur_kernel, chunk=chunk, has_state=has_state),
        grid=(B, nc),
        in_specs=in_specs,
        out_specs=[qkv_spec] + st_specs,
        out_shape=[jax.ShapeDtypeStruct((B, T, INNER), BF16),
                   jax.ShapeDtypeStruct((B, N_HEADS, d, d), F32),
                   jax.ShapeDtypeStruct((B, N_HEADS, 1, d), F32),
                   jax.ShapeDtypeStruct((B, N_HEADS, 1, 1), F32)],
        compiler_params=_cparams(("parallel", "arbitrary")),
        name="mlstm_recurrence",
    )(*args)


STATE_ROWS = HEAD_DIM + 16


def _recur_t_kernel(q_ref, k_ref, vt_ref, g_ref, ac_ref, hn_ref, c_out, n_out, m_out, cx_sc, m_sc, *, chunk):
    L = chunk
    d = HEAD_DIM

    @pl.when(pl.program_id(1) == 0)
    def _():
        cx_sc[...] = jnp.zeros_like(cx_sc)
        m_sc[...] = jnp.zeros_like(m_sc)

    si = lax.broadcasted_iota(jnp.int32, (L, L), 0)
    ti = lax.broadcasted_iota(jnp.int32, (L, L), 1)
    causal = si <= ti

    a_rows = g_ref[0, 0:N_HEADS]
    b_rows = g_ref[0, N_HEADS:2 * N_HEADS]
    cma_rows = g_ref[0, 2 * N_HEADS:3 * N_HEADS]
    a_cols = ac_ref[0]

    def scores(h):
        q = q_ref[0, :, h * d:(h + 1) * d]
        st_raw = _dot_nt(k_ref[0, :, h * d:(h + 1) * d], q)
        cq = _dot_nt(cx_sc[h].astype(BF16), q)
        return st_raw, cq, m_sc[h]

    def update(h, m):
        a_row = a_rows[h:h + 1]
        b_last = b_rows[h:h + 1, L - 1:L]
        a_max = cma_rows[h:h + 1, L - 1:L]
        big_n = jnp.maximum(m, a_max)
        decay = jnp.exp(m - big_n)
        w_upd = jnp.exp(a_row - big_n)
        vw = vt_ref[0, h * d:(h + 1) * d, :].astype(F32) * w_upd
        ext = jnp.concatenate([vw, jnp.broadcast_to(w_upd, (STATE_ROWS - d, L))], axis=0).astype(BF16)
        cx_sc[h] = decay * cx_sc[h] + _dot(ext, k_ref[0, :, h * d:(h + 1) * d])
        m_sc[h] = b_last + big_n

    def outputs(h, st_raw, cq, m):
        a_col = a_cols[:, h:h + 1]
        b_row = b_rows[h:h + 1]
        cma_row = cma_rows[h:h + 1]
        big_m = jnp.maximum(m, cma_row)
        w_intra = jnp.exp(jnp.where(causal, a_col - big_m, NEG_INF))
        w_inter = jnp.exp(m - big_m)
        st = st_raw * w_intra
        num = w_inter * cq[:d] + _dot(vt_ref[0, h * d:(h + 1) * d, :], st.astype(BF16))
        den = w_inter * cq[d:d + 1] + jnp.sum(st, axis=0, keepdims=True)
        hh = num * (1.0 / jnp.maximum(jnp.abs(den), jnp.exp(-(b_row + big_m))))
        mu = jnp.mean(hh, axis=0, keepdims=True)
        hc = hh - mu
        var = jnp.mean(hc * hc, axis=0, keepdims=True)
        hn_ref[0, :, h * d:(h + 1) * d] = (hc * lax.rsqrt(var + EPS)).astype(BF16).T

    pending = {0: scores(0)}
    for h in range(N_HEADS):
        if h + 1 < N_HEADS:
            pending[h + 1] = scores(h + 1)
        st_raw, cq, m = pending.pop(h)
        update(h, m)
        outputs(h, st_raw, cq, m)

    @pl.when(pl.program_id(1) == pl.num_programs(1) - 1)
    def _():
        c_out[0] = cx_sc[:, :d, :]
        n_out[0] = cx_sc[:, d:d + 1, :]
        m_out[0] = m_sc[...]


def _recurrence_t(q, k, vt, gate_rows, a_cols, *, chunk):
    B, T, _ = q.shape
    nc = T // chunk
    d = HEAD_DIM
    qk_spec = pl.BlockSpec((1, chunk, INNER), lambda b, c: (b, c, 0))
    return pl.pallas_call(
        functools.partial(_recur_t_kernel, chunk=chunk),
        grid=(B, nc),
        in_specs=[qk_spec, qk_spec,
                  pl.BlockSpec((1, INNER, chunk), lambda b, c: (b, 0, c)),
                  pl.BlockSpec((1, 3 * N_HEADS, chunk), lambda b, c: (b, 0, c)),
                  pl.BlockSpec((1, chunk, 2 * N_HEADS), lambda b, c: (b, c, 0))],
        out_specs=[qk_spec,
                   pl.BlockSpec((1, N_HEADS, d, d), lambda b, c: (b, 0, 0, 0)),
                   pl.BlockSpec((1, N_HEADS, 1, d), lambda b, c: (b, 0, 0, 0)),
                   pl.BlockSpec((1, N_HEADS, 1, 1), lambda b, c: (b, 0, 0, 0))],
        out_shape=[jax.ShapeDtypeStruct((B, T, INNER), BF16),
                   jax.ShapeDtypeStruct((B, N_HEADS, d, d), F32),
                   jax.ShapeDtypeStruct((B, N_HEADS, 1, d), F32),
                   jax.ShapeDtypeStruct((B, N_HEADS, 1, 1), F32)],
        scratch_shapes=[pltpu.VMEM((N_HEADS, STATE_ROWS, d), F32), pltpu.VMEM((N_HEADS, 1, 1), F32)],
        compiler_params=_cparams(("parallel", "arbitrary")),
        name="mlstm_recurrence_t",
    )(q, k, vt, gate_rows, a_cols)


def _mid_kernel(hn_ref, xc_ref, sz_ref, x_ref, cos_ref, sin_ref,
                onw_ref, skip_ref, wout_ref, kvnw_ref, dkv_ref, latw_ref,
                bnw_ref, bwin_ref, qnw_ref, wuq_ref, wuk_ref,
                x1_ref, ckv_ref, kpe_ref, kcat_ref, qcat_ref, gate_ref, *maybe_vt_ref):
    cos_t = cos_ref[...]
    sin_t = sin_ref[...]
    y = (hn_ref[0].astype(F32) * onw_ref[...] + skip_ref[...] * xc_ref[0].astype(F32)) * sz_ref[0].astype(F32)
    x1 = x_ref[0] + _dot(y.astype(BF16), wout_ref[...])
    x1_ref[0] = x1

    ukv = _dot(_rms(x1, kvnw_ref[...]).astype(BF16), dkv_ref[...])
    ckv = _rms(ukv[:, :KV_RANK], latw_ref[...])
    kpe = _rope128(ukv[:, KV_RANK:], cos_t, sin_t)
    ckv_ref[0] = ckv
    kpe_ref[0] = kpe[:, :QK_ROPE]
    kcat_ref[0, :, :KV_RANK] = ckv.astype(BF16)
    kcat_ref[0, :, KV_RANK:] = kpe.astype(BF16)
    for vt_ref in maybe_vt_ref:
        vt_ref[0, 0] = ckv.astype(BF16).T

    ub =_dot(_rms(x1, bnw_ref[...]).astype(BF16), bwin_ref[...])
    gate_ref[0] = _silu(ub[:, Q_RANK:]).astype(BF16)
    cq = _rms(ub[:, :Q_RANK], qnw_ref[...]).astype(BF16)
    qq = _dot(cq, wuq_ref[...])
    nope_w = MLA_HEADS * QK_NOPE
    for h in range(MLA_HEADS):
        q_lat = _dot(qq[:, h * QK_NOPE:(h + 1) * QK_NOPE].astype(BF16), wuk_ref[h])
        q_pe = _rope128(qq[:, nope_w + h * LANE:nope_w + (h + 1) * LANE], cos_t, sin_t)
        qcat_ref[0, h, :, :KV_RANK] = q_lat.astype(BF16)
        qcat_ref[0, h, :, KV_RANK:] = q_pe.astype(BF16)


def _mid(hn, xc, sz, x, cos_t, sin_t, onw, skip, wout, kvnw, dkv, latw, bnw, bwin, qnw, wuq, wuk, *, tt, emit_vt):
    B, T, _ = x.shape
    nt = T // tt
    tok = lambda w: pl.BlockSpec((1, tt, w), lambda b, t: (b, t, 0))
    pos = pl.BlockSpec((tt, LANE), lambda b, t: (t, 0))
    vt_specs = [pl.BlockSpec((1, 1, KV_RANK, tt), lambda b, t: (b, t, 0, 0))] if emit_vt else []
    vt_shapes = [jax.ShapeDtypeStruct((B, nt, KV_RANK, tt), BF16)] if emit_vt else []
    return pl.pallas_call(
        _mid_kernel,
        grid=(B, nt),
        in_specs=[tok(INNER), tok(INNER), tok(INNER), tok(D_MODEL), pos, pos,
                  _const_spec((1, INNER)), _const_spec((1, INNER)), _const_spec((INNER, D_MODEL)),
                  _const_spec((1, D_MODEL)), _const_spec((D_MODEL, KCAT)), _const_spec((1, KV_RANK)),
                  _const_spec((1, D_MODEL)), _const_spec((D_MODEL, Q_RANK + GATE_W)), _const_spec((1, Q_RANK)),
                  _const_spec((Q_RANK, 2 * MLA_HEADS * LANE)), _const_spec((MLA_HEADS, QK_NOPE, KV_RANK))],
        out_specs=[tok(D_MODEL), tok(KV_RANK), tok(QK_ROPE), tok(KCAT),
                   pl.BlockSpec((1, MLA_HEADS, tt, KCAT), lambda b, t: (b, 0, t, 0)),
                   tok(GATE_W)] + vt_specs,
        out_shape=[jax.ShapeDtypeStruct((B, T, D_MODEL), F32),
                   jax.ShapeDtypeStruct((B, T, KV_RANK), F32),
                   jax.ShapeDtypeStruct((B, T, QK_ROPE), F32),
                   jax.ShapeDtypeStruct((B, T, KCAT), BF16),
                   jax.ShapeDtypeStruct((B, MLA_HEADS, T, KCAT), BF16),
                   jax.ShapeDtypeStruct((B, T, GATE_W), BF16)] + vt_shapes,
        compiler_params=_cparams(("parallel", "parallel")),
        name="mid_proj",
    )(hn, xc, sz, x, cos_t, sin_t, onw, skip, wout, kvnw, dkv, latw, bnw, bwin, qnw, wuq, wuk)


def _attn_prompt_kernel(q_ref, k_ref, vt_ref, gate_ref, x1_ref, wuv_ref, wout_ref, fnw_ref, y_ref, acc_sc, *, tq):
    i = pl.program_id(1)
    ahead = 4

    def run(groups, m, l, first):
        m, l = list(m), list(l)
        keys, vals = [], []
        for b0, nb in groups:
            keys.append(k_ref[0, pl.ds(pl.multiple_of(b0 * tq, tq), nb * tq), :])
            vals.append(jnp.concatenate([vt_ref[0, b0 + n] for n in range(nb)], axis=1) if nb > 1
                        else vt_ref[0, b0])
        if first:
            tk = groups[0][1] * tq
            ks = lax.broadcasted_iota(jnp.int32, (tk, tq), 0) - (tk - tq)
            qs = lax.broadcasted_iota(jnp.int32, (tk, tq), 1)
            visible = (ks < 0) | ((ks // MASK_CHUNK) <= (qs // MASK_CHUNK))
        units = [(g, h) for g in range(len(groups)) for h in range(MLA_HEADS)]
        raw = [_dot_nt(keys[g], q_ref[0, h]) for g, h in units[:ahead]]
        for u, (g, h) in enumerate(units):
            if u + ahead < len(units):
                g2, h2 = units[u + ahead]
                raw.append(_dot_nt(keys[g2], q_ref[0, h2]))
            s = raw[u] * MLA_SCALE
            raw[u] = None
            if first:
                s = jnp.where(visible, s, NEG_INF)
                m[h] = jnp.max(s, axis=0, keepdims=True)
                p = jnp.exp(s - m[h])
                acc_sc[h] = _dot(vals[g], p.astype(BF16))
                l[h] = jnp.sum(p, axis=0, keepdims=True)
            else:
                m_new = jnp.maximum(m[h], jnp.max(s, axis=0, keepdims=True))
                alpha = jnp.exp(m[h] - m_new)
                p = jnp.exp(s - m_new)
                acc_sc[h] = alpha * acc_sc[h] + _dot(vals[g], p.astype(BF16))
                l[h] = alpha * l[h] + jnp.sum(p, axis=0, keepdims=True)
                m[h] = m_new
        return tuple(m), tuple(l)

    none = (None,) * MLA_HEADS
    m, l = lax.cond(i % 2 == 1,
                    lambda: run([(i - 1, 2)], none, none, True),
                    lambda: run([(i, 1)], none, none, True))
    pairs = i // 2
    m, l = lax.cond(pairs % 2 == 1,
                    lambda: run([(2 * (pairs - 1), 2)], m, l, False),
                    lambda: (m, l))
    m, l = lax.fori_loop(0, pairs // 2, lambda j, c: run([(4 * j, 2), (4 * j + 2, 2)], c[0], c[1], False), (m, l))
    o_heads = [(acc_sc[h] / l[h]).astype(BF16).T for h in range(MLA_HEADS)]
    y_ref[0] = _final_math(o_heads, gate_ref[0], x1_ref[0], wuv_ref, wout_ref, fnw_ref)


def _attn_prompt(qcat, kcat, vt, gate, x1, wuv, wout, fnw, *, tq):
    B, _, T, _ = qcat.shape
    nk = T // tq
    tok = lambda w: pl.BlockSpec((1, tq, w), lambda b, i: (b, i, 0))
    return pl.pallas_call(
        functools.partial(_attn_prompt_kernel, tq=tq),
        grid=(B, T // tq),
        in_specs=[pl.BlockSpec((1, MLA_HEADS, tq, KCAT), lambda b, i: (b, 0, i, 0)),
                  pl.BlockSpec((1, T, KCAT), lambda b, i: (b, 0, 0)),
                  pl.BlockSpec((1, nk, KV_RANK, tq), lambda b, i: (b, 0, 0, 0)),
                  tok(GATE_W), tok(D_MODEL),
                  _const_spec((MLA_HEADS, KV_RANK, V_HEAD)), _const_spec((GATE_W, D_MODEL)),
                  _const_spec((1, D_MODEL))],
        out_specs=tok(D_MODEL),
        out_shape=jax.ShapeDtypeStruct((B, T, D_MODEL), F32),
        scratch_shapes=[pltpu.VMEM((MLA_HEADS, KV_RANK, tq), F32)],
        compiler_params=_cparams(("parallel", "arbitrary")),
        name="mla_attention_prompt",
    )(qcat, kcat, vt, gate, x1, wuv, wout, fnw)


def _attn_cached_kernel(q_ref, pckv_ref, pkpe_ref, knew_ref, o_ref, *, tq, past):
    rows = MLA_HEADS * tq
    q = q_ref[0].reshape(rows, KCAT)
    ckv = pckv_ref[0].astype(BF16)
    kpe = pkpe_ref[0].astype(BF16)
    knew = knew_ref[0]
    s_old = (_dot_nt(q[:, :KV_RANK], ckv) + _dot_nt(q[:, KV_RANK:KV_RANK + QK_ROPE], kpe)) * MLA_SCALE
    s_new = _dot_nt(q, knew) * MLA_SCALE

    def masked(s, key_pos0):
        r = lax.broadcasted_iota(jnp.int32, s.shape, 0)
        key_pos = key_pos0 + lax.broadcasted_iota(jnp.int32, s.shape, 1)
        q_pos = past + (r & (tq - 1))
        return jnp.where((key_pos // MASK_CHUNK) <= (q_pos // MASK_CHUNK), s, NEG_INF)

    s_old = masked(s_old, 0)
    s_new = masked(s_new, past)
    m = jnp.maximum(jnp.max(s_old, axis=-1, keepdims=True), jnp.max(s_new, axis=-1, keepdims=True))
    p_old = jnp.exp(s_old - m)
    p_new = jnp.exp(s_new - m)
    l = jnp.sum(p_old, axis=-1, keepdims=True) + jnp.sum(p_new, axis=-1, keepdims=True)
    o = (_dot(p_old.astype(BF16), ckv) + _dot(p_new.astype(BF16), knew[:, :KV_RANK])) / l
    o_ref[0] = o.astype(BF16).reshape(MLA_HEADS, tq, KV_RANK)


def _attn_cached(qcat, past_ckv, past_kpe, kcat):
    B, _, tq, _ = qcat.shape
    past = past_ckv.shape[1]
    return pl.pallas_call(
        functools.partial(_attn_cached_kernel, tq=tq, past=past),
        grid=(B,),
        in_specs=[pl.BlockSpec((1, MLA_HEADS, tq, KCAT), lambda b: (b, 0, 0, 0)),
                  pl.BlockSpec((1, past, KV_RANK), lambda b: (b, 0, 0)),
                  pl.BlockSpec((1, past, QK_ROPE), lambda b: (b, 0, 0)),
                  pl.BlockSpec((1, tq, KCAT), lambda b: (b, 0, 0))],
        out_specs=pl.BlockSpec((1, MLA_HEADS, tq, KV_RANK), lambda b: (b, 0, 0, 0)),
        out_shape=jax.ShapeDtypeStruct((B, MLA_HEADS, tq, KV_RANK), BF16),
        compiler_params=_cparams(("parallel",)),
        name="mla_attention_cached",
    )(qcat, past_ckv, past_kpe, kcat)


def _final_math(o_heads, gate, x1, wuv_ref, wout_ref, fnw_ref):
    o = jnp.concatenate([_dot(o_heads[h], wuv_ref[h]) for h in range(MLA_HEADS)], axis=-1)
    og = (o * gate.astype(F32)).astype(BF16)
    x2 = x1 + _dot(og, wout_ref[...])
    return _rms(x2, fnw_ref[...])


def _final_kernel(o_ref, gate_ref, x1_ref, wuv_ref, wout_ref, fnw_ref, y_ref):
    o_heads = [o_ref[0, h] for h in range(MLA_HEADS)]
    y_ref[0] = _final_math(o_heads, gate_ref[0], x1_ref[0], wuv_ref, wout_ref, fnw_ref)


def _final(o_lat, gate, x1, wuv, wout, fnw, *, tt):
    B, T, _ = x1.shape
    tok = lambda w: pl.BlockSpec((1, tt, w), lambda b, t: (b, t, 0))
    return pl.pallas_call(
        _final_kernel,
        grid=(B, T // tt),
        in_specs=[pl.BlockSpec((1, MLA_HEADS, tt, KV_RANK), lambda b, t: (b, 0, t, 0)),
                  tok(GATE_W), tok(D_MODEL),
                  _const_spec((MLA_HEADS, KV_RANK, V_HEAD)), _const_spec((GATE_W, D_MODEL)),
                  _const_spec((1, D_MODEL))],
        out_specs=tok(D_MODEL),
        out_shape=jax.ShapeDtypeStruct((B, T, D_MODEL), F32),
        compiler_params=_cparams(("parallel", "parallel")),
        name="mla_final",
    )(o_lat, gate, x1, wuv, wout, fnw)


def _block_diag(w):
    g = BD // QKV_BLOCK
    nb = INNER // BD
    wb = w.reshape(nb, g, QKV_BLOCK, QKV_BLOCK)
    eye = jnp.eye(g, dtype=w.dtype)
    dense = wb[:, :, :, None, :] * eye[None, :, None, :, None]
    return dense.reshape(nb, BD, BD).astype(BF16)


def _fold_gates(w_head, w_gate):
    wg = w_gate.reshape(INNER // QKV_BLOCK, QKV_BLOCK, 2 * N_HEADS)
    return jnp.einsum("gio,gon->gin", w_head, wg, precision=lax.Precision.HIGHEST).reshape(INNER, 2 * N_HEADS)


def _pad_lanes(w, width):
    return jnp.pad(w, ((0, 0),) * (w.ndim - 1) + ((0, width - w.shape[-1]),))


def _prep_weights(a_norm_w, a_w_in, a_conv_w, a_conv_b, a_w_q, a_w_k, a_w_v, a_w_gates, a_b_gates, a_outnorm_w,
                  a_skip, a_w_out, kv_norm_w, kv_w_dkv, kv_latent_norm_w, kv_w_uk, kv_w_uv,
                  b_norm_w, b_w_in, b_q_norm_w, b_w_uq, b_w_out, final_norm_w):
    l = 0
    wg = a_w_gates[l]
    wgc = _fold_gates(a_w_q[l], wg[:INNER]) + _fold_gates(a_w_k[l], wg[INNER:2 * INNER])
    wgi = _fold_gates(a_w_v[l], wg[2 * INNER:])
    gate_rows = lambda m: jnp.pad(m.T, ((0, GATE_ROWS - 2 * N_HEADS), (0, 0)))
    wuq =b_w_uq[l].reshape(Q_RANK, MLA_HEADS, QK_NOPE + QK_ROPE)
    wuq_nope = wuq[:, :, :QK_NOPE].reshape(Q_RANK, MLA_HEADS * QK_NOPE)
    wuq_rope = jnp.pad(wuq[:, :, QK_NOPE:], ((0, 0), (0, 0), (0, LANE - QK_ROPE))).reshape(Q_RANK, MLA_HEADS * LANE)
    return dict(
        nw=a_norm_w[l][None], win=a_w_in[l].astype(BF16), cw=a_conv_w[l], cb=a_conv_b[l][None],
        bdq=_block_diag(a_w_q[l]), bdk=_block_diag(a_w_k[l]), bdv=_block_diag(a_w_v[l]),
        bdv_t=jnp.swapaxes(_block_diag(a_w_v[l]), 1, 2),
        wgc=_pad_lanes(wgc, LANE).astype(BF16), wgi=_pad_lanes(wgi, LANE).astype(BF16),
        bg=_pad_lanes(a_b_gates[l][None], LANE),
        wgc_t=gate_rows(wgc).astype(BF16), wgi_t=gate_rows(wgi).astype(BF16), bg_t=gate_rows(a_b_gates[l][None]),
        onw=a_outnorm_w[l][None], skip=a_skip[l][None], wout_a=a_w_out[l].astype(BF16),
        kvnw=kv_norm_w[None], dkv=_pad_lanes(kv_w_dkv, KCAT).astype(BF16), latw=kv_latent_norm_w[None],
        bnw=b_norm_w[l][None], bwin=b_w_in[l].astype(BF16), qnw=b_q_norm_w[l][None],
        wuq=jnp.concatenate([wuq_nope, wuq_rope], axis=1).astype(BF16),
        wuk=jnp.transpose(kv_w_uk, (1, 2, 0)).astype(BF16),
        wuv=jnp.transpose(kv_w_uv, (1, 0, 2)).astype(BF16),
        wout_b=b_w_out[l].astype(BF16), fnw=final_norm_w[None],
    )


def _rope_tables(pos):
    half = QK_ROPE // 2
    inv = ROPE_BASE ** (-jnp.arange(half, dtype=F32) / half)
    ang = pos.astype(F32)[:, None] * inv[None, :]
    cos, sin = jnp.cos(ang), jnp.sin(ang)
    zero = jnp.zeros((pos.shape[0], LANE - QK_ROPE), F32)
    return jnp.concatenate([cos, cos, zero], axis=1), jnp.concatenate([-sin, sin, zero], axis=1)


def _trunk(x, pos0, state, conv_buf, cache, w, *, tt, chunk):
    B, T, _ = x.shape
    time_on_lanes = state is None and T % LANE == 0
    gw = (w["bdv_t"], w["wgc_t"], w["wgi_t"], w["bg_t"]) if time_on_lanes else (w["bdv"], w["wgc"], w["wgi"], w["bg"])
    q, k, v, xc, sz, g, conv_tail, *a_cols = _front(x, conv_buf, w["nw"], w["win"], w["cw"], w["cb"], w["bdq"],
                                                    w["bdk"], *gw, tt=tt, v_transposed=time_on_lanes)
    if time_on_lanes:
        assert chunk == tt
        hn, c_new, n_new, m_new = _recurrence_t(q, k, v, g, a_cols[0], chunk=chunk)
    else:
        gt = jnp.transpose(g, (0, 2, 1))
        tp = max(T, LANE)
        if tp != T:
            gt = jnp.pad(gt, ((0, 0), (0, 0), (0, tp - T)))
        r = _gate_prep(gt, chunk=chunk)
        rc = jnp.transpose(r[..., :T], (0, 1, 3, 2))
        if state is not None:
            c0, n0, m0 = state
            state = (c0, n0[:, :, None, :], m0[:, :, None, None])
        hn, c_new, n_new, m_new = _recurrence(q, k, v, r, rc, state, chunk=chunk)

    cos_t, sin_t = _rope_tables(pos0 + jnp.arange(T, dtype=jnp.int32))
    x1, ckv, kpe, kcat, qcat, gate, *vt = _mid(hn, xc, sz, x, cos_t, sin_t, w["onw"], w["skip"], w["wout_a"],
                                               w["kvnw"], w["dkv"], w["latw"], w["bnw"], w["bwin"], w["qnw"],
                                               w["wuq"], w["wuk"], tt=tt, emit_vt=cache is None)
    if cache is None:
        y = _attn_prompt(qcat, kcat, vt[0], gate, x1, w["wuv"], w["wout_b"], w["fnw"], tq=tt)
    else:
        past_ckv, past_kpe = cache
        assert pos0 == past_ckv.shape[1]
        o_lat = _attn_cached(qcat, past_ckv, past_kpe, kcat)
        y = _final(o_lat, gate, x1, w["wuv"], w["wout_b"], w["fnw"], tt=tt)
    return (y, c_new[None], n_new[None, :, :, 0, :], m_new[None, :, :, 0, 0], conv_tail[None], ckv, kpe)


def kernel(x_prompt, x_sample, state_mlstm_C, state_mlstm_n, state_mlstm_m, state_mlstm_conv, cache_mla_ckv,
           cache_mla_kpe, a_norm_w, a_w_in, a_conv_w, a_conv_b, a_w_q, a_w_k, a_w_v, a_w_gates, a_b_gates,
           a_outnorm_w, a_skip, a_w_out, kv_norm_w, kv_w_dkv, kv_latent_norm_w, kv_w_uk, kv_w_uv, b_norm_w, b_w_in,
           b_q_norm_w, b_w_uq, b_w_out, final_norm_w):
    w = _prep_weights(a_norm_w, a_w_in, a_conv_w, a_conv_b, a_w_q, a_w_k, a_w_v, a_w_gates, a_b_gates, a_outnorm_w,
                      a_skip, a_w_out, kv_norm_w, kv_w_dkv, kv_latent_norm_w, kv_w_uk, kv_w_uv,
                      b_norm_w, b_w_in, b_q_norm_w, b_w_uq, b_w_out, final_norm_w)
    bp, tp, _ = x_prompt.shape
    bs, ts, _ = x_sample.shape
    past = cache_mla_ckv.shape[1]
    zero_buf = jnp.zeros((bp, CONV_WIDTH - 1, INNER), F32)
    out_p = _trunk(x_prompt, 0, None, zero_buf, None, w, tt=256, chunk=256)
    out_s = _trunk(x_sample, past, (state_mlstm_C[0], state_mlstm_n[0], state_mlstm_m[0]), state_mlstm_conv[0],
                   (cache_mla_ckv, cache_mla_kpe), w, tt=ts, chunk=ts)
    return (out_p[0], out_s[0]) + tuple(out_p[1:]) + tuple(out_s[1:])
```

```python
import functools
import math

import jax
import jax.numpy as jnp
from jax import lax
from jax.experimental import pallas as pl
from jax.experimental.pallas import tpu as pltpu

F32 = jnp.float32
BF16 = jnp.bfloat16

D_MODEL = 1024
INNER = 2 * D_MODEL
N_HEADS = 4
HEAD_DIM = INNER // N_HEADS
QKV_BLOCK = 4
CONV_WIDTH = 4
MLA_HEADS = 8
QK_NOPE = 128
QK_ROPE = 64
V_HEAD = 128
KV_RANK = 256
Q_RANK = 384
GATE_W = MLA_HEADS * V_HEAD
MLA_SCALE = 1.0 / math.sqrt(QK_NOPE + QK_ROPE)
ROPE_BASE = 10000.0
MASK_CHUNK = 64
EPS = 1e-6

LANE = 128
GATE_ROWS = 16
BD = 256
KCAT = KV_RANK + LANE
VMEM_LIMIT = 56 * 1024 * 1024

NEG_INF = float("-inf")


def _cparams(sem):
    return pltpu.CompilerParams(dimension_semantics=sem, vmem_limit_bytes=VMEM_LIMIT)


def _const_spec(shape):
    nd = len(shape)
    return pl.BlockSpec(shape, lambda *_: (0,) * nd, pipeline_mode=pl.Buffered(1))


def _rms(x, w):
    return x * lax.rsqrt(jnp.mean(x * x, axis=-1, keepdims=True) + EPS) * w


def _silu(x):
    return x / (1.0 + jnp.exp(-x))


def _dot(a, b):
    return jnp.dot(a, b, preferred_element_type=F32)


def _dot_nt(a, b):
    return lax.dot_general(a, b, (((1,), (1,)), ((), ())), preferred_element_type=F32)


def _dot_tn(a, b):
    return lax.dot_general(a, b, (((0,), (0,)), ((), ())), preferred_element_type=F32)


def _rope128(x, cos_t, sin_t):
    lane = lax.broadcasted_iota(jnp.int32, x.shape, 1)
    partner = jnp.where(lane < QK_ROPE // 2, pltpu.roll(x, LANE - QK_ROPE // 2, 1), pltpu.roll(x, QK_ROPE // 2, 1))
    return x * cos_t + partner * sin_t


def _front_kernel(x_ref, buf_ref, nw_ref, win_ref, cw_ref, cb_ref, bdq_ref, bdk_ref, bdv_ref,
                  wgc_ref, wgi_ref, bg_ref,
                  q_ref, k_ref, v_ref, xc_ref, sz_ref, g_ref, tail_ref, *rest, nb, tt, v_transposed):
    xs_ref = rest[-1]
    t = pl.program_id(1)
    halo = CONV_WIDTH - 1
    rows = nb * tt

    @pl.when(t == 0)
    def _():
        xs_ref[:, 8 - halo:8, :] = buf_ref[...]

    xnb = _rms(x_ref[...].reshape(rows, D_MODEL), nw_ref[...]).astype(BF16)
    x_in = _dot(xnb, win_ref[:, :INNER])
    xs_ref[:, 8:8 + tt, :] = x_in.reshape(nb, tt, INNER)

    cw = cw_ref[...]
    conv = cb_ref[...] + x_in * cw[halo:halo + 1]
    for j in range(halo):
        conv = conv + xs_ref[:, 8 - halo + j:8 - halo + j + tt, :].reshape(rows, INNER) * cw[j:j + 1]
    tail_ref[...] = xs_ref[:, 8 + tt - halo:8 + tt, :]
    xs_ref[:, 0:8, :] = xs_ref[:, tt:tt + 8, :]

    xcb = _silu(conv).astype(BF16)
    xib = x_in.astype(BF16)
    xc_ref[...] = xcb.reshape(nb, tt, INNER)

    if v_transposed:
        g = bg_ref[...] + _dot_nt(wgc_ref[...], xcb) + _dot_nt(wgi_ref[...], xib)
        a, b, cma = _gate_rows(g[:2 * N_HEADS], tt)
        g_ref[0, 0:N_HEADS] = a
        g_ref[0, N_HEADS:2 * N_HEADS] = b
        g_ref[0, 2 * N_HEADS:3 * N_HEADS] = cma
    else:
        g = bg_ref[...] + _dot(xcb, wgc_ref[...]) + _dot(xib, wgi_ref[...])
        g_ref[...] = g[:, :2 * N_HEADS].reshape(nb, tt, 2 * N_HEADS)
    sz_ref[...] = _silu(_dot(xnb, win_ref[:, INNER:])).astype(BF16).reshape(nb, tt, INNER)
    for j in range(INNER // BD):
        sl = slice(j * BD, (j + 1) * BD)
        q_ref[:, :, sl] = _dot(xcb[:, sl], bdq_ref[j]).astype(BF16).reshape(nb, tt, BD)
        k_ref[:, :, sl] = (_dot(xcb[:, sl], bdk_ref[j]) * (HEAD_DIM ** -0.5)).astype(BF16).reshape(nb, tt, BD)
        if v_transposed:
            v_ref[0, sl, :] = _dot_nt(bdv_ref[j], xib[:, sl]).astype(BF16)
        else:
            v_ref[:, :, sl] = _dot(xib[:, sl], bdv_ref[j]).astype(BF16).reshape(nb, tt, BD)
    if v_transposed:
        rest[0][0] = _rows_to_cols(jnp.concatenate([a, a], axis=0))


def _front(x, buf, nw, win, cw, cb, bdq, bdk, bdv, wgc, wgi, bg, *, nb, tt, v_transposed):
    B, T, _ = x.shape
    assert B % nb == 0 and T % tt == 0 and (nb == 1 or not v_transposed)
    act = jax.ShapeDtypeStruct((B, T, INNER), BF16)
    tok = lambda w: pl.BlockSpec((nb, tt, w), lambda b, t: (b, t, 0))
    hist = pl.BlockSpec((nb, CONV_WIDTH - 1, INNER), lambda b, t: (b, 0, 0))
    if v_transposed:
        v_spec = pl.BlockSpec((1, INNER, tt), lambda b, t: (b, 0, t))
        v_shape = jax.ShapeDtypeStruct((B, INNER, T), BF16)
        g_spec = pl.BlockSpec((1, 3 * N_HEADS, tt), lambda b, t: (b, 0, t))
        g_shape = jax.ShapeDtypeStruct((B, 3 * N_HEADS, T), F32)
        gate_w_specs = [_const_spec((GATE_ROWS, INNER)), _const_spec((GATE_ROWS, INNER)), _const_spec((GATE_ROWS, 1))]
        extra_specs, extra_shapes = [tok(2 * N_HEADS)], [jax.ShapeDtypeStruct((B, T, 2 * N_HEADS), F32)]
    else:
        extra_specs, extra_shapes = [], []
        v_spec, v_shape = tok(INNER), act
        g_spec, g_shape = tok(2 * N_HEADS), jax.ShapeDtypeStruct((B, T, 2 * N_HEADS), F32)
        gate_w_specs = [_const_spec((INNER, LANE)), _const_spec((INNER, LANE)), _const_spec((1, LANE))]
    return pl.pallas_call(
        functools.partial(_front_kernel, nb=nb, tt=tt, v_transposed=v_transposed),
        grid=(B // nb, T // tt),
        in_specs=[tok(D_MODEL), hist,
                  _const_spec((1, D_MODEL)), _const_spec((D_MODEL, 2 * INNER)),
                  _const_spec((CONV_WIDTH, INNER)), _const_spec((1, INNER)),
                  _const_spec((INNER // BD, BD, BD)), _const_spec((INNER // BD, BD, BD)),
                  _const_spec((INNER // BD, BD, BD)),
                  ] + gate_w_specs,
        out_specs=[tok(INNER), tok(INNER), v_spec, tok(INNER), tok(INNER), g_spec, hist] + extra_specs,
        out_shape=[act, act, v_shape, act, act, g_shape,
                   jax.ShapeDtypeStruct((B, CONV_WIDTH - 1, INNER), F32)] + extra_shapes,
        scratch_shapes=[pltpu.VMEM((nb, 8 + tt, INNER), F32)],
        compiler_params=_cparams(("parallel", "arbitrary")),
        name="mlstm_front",
    )(x, buf, nw, win, cw, cb, bdq, bdk, bdv, wgc, wgi, bg)


def _gate_rows(g, chunk):
    ig = g[:N_HEADS]
    fg = g[N_HEADS:]
    lf = jnp.minimum(fg, 0.0) - jnp.log(1.0 + jnp.exp(-jnp.abs(fg)))
    pos = lax.broadcasted_iota(jnp.int32, lf.shape, 1) & (chunk - 1)
    b = lf
    sh = 1
    while sh < chunk:
        b = b + jnp.where(pos >= sh, pltpu.roll(b, sh, 1), 0.0)
        sh *= 2
    a = ig - b
    cma = a
    sh = 1
    while sh < chunk:
        cma = jnp.maximum(cma, jnp.where(pos >= sh, pltpu.roll(cma, sh, 1), NEG_INF))
        sh *= 2
    return a, b, cma


def _rows_to_cols(rows):
    n = rows.shape[1]
    eye = (lax.broadcasted_iota(jnp.int32, (n, n), 0) == lax.broadcasted_iota(jnp.int32, (n, n), 1))
    eye = jnp.where(eye, 1.0, 0.0).astype(BF16)
    p1 = rows.astype(BF16)
    r1 = rows - p1.astype(F32)
    p2 = r1.astype(BF16)
    p3 = (r1 - p2.astype(F32)).astype(BF16)
    return (_dot_nt(eye, p1) + _dot_nt(eye, p2)) + _dot_nt(eye, p3)


def _gate_prep_kernel(gt_ref, r_ref, *, chunk):
    a, b, cma = _gate_rows(gt_ref[0], chunk)
    zero = jnp.zeros_like(a[0:1])
    for h in range(N_HEADS):
        r_ref[0, h] = jnp.concatenate([a[h:h + 1], b[h:h + 1], cma[h:h + 1]] + [zero] * 5, axis=0)


def _gate_prep(gt, *, chunk):
    B, _, Tp = gt.shape
    return pl.pallas_call(
        functools.partial(_gate_prep_kernel, chunk=chunk),
        grid=(B,),
        in_specs=[pl.BlockSpec((1, 2 * N_HEADS, Tp), lambda b: (b, 0, 0))],
        out_specs=pl.BlockSpec((1, N_HEADS, 8, Tp), lambda b: (b, 0, 0, 0)),
        out_shape=jax.ShapeDtypeStruct((B, N_HEADS, 8, Tp), F32),
        compiler_params=_cparams(("parallel",)),
        name="mlstm_gate_prep",
    )(gt)


def _recur_kernel(*refs, chunk, has_state):
    if has_state:
        q_ref, k_ref, v_ref, r_ref, rc_ref, c0_ref, n0_ref, m0_ref, hn_ref, c_st, n_st, m_st = refs
    else:
        q_ref, k_ref, v_ref, r_ref, rc_ref, hn_ref, c_st, n_st, m_st = refs
    L = chunk
    d = HEAD_DIM

    @pl.when(pl.program_id(1) == 0)
    def _():
        if has_state:
            c_st[...] = c0_ref[...]
            n_st[...] = n0_ref[...]
            m_st[...] = m0_ref[...]
        else:
            c_st[...] = jnp.zeros_like(c_st)
            n_st[...] = jnp.zeros_like(n_st)
            m_st[...] = jnp.zeros_like(m_st)

    ti = lax.broadcasted_iota(jnp.int32, (L, L), 0)
    si = lax.broadcasted_iota(jnp.int32, (L, L), 1)
    causal = si <= ti

    def head_slice(ref, h):
        return ref[0, :, h * d:(h + 1) * d]

    def scores(h):
        q = head_slice(q_ref, h)
        s_raw = _dot_nt(q, head_slice(k_ref, h))
        qc = _dot_nt(q, c_st[0, h].astype(BF16))
        qn = jnp.sum(q.astype(F32) * n_st[0, h], axis=-1, keepdims=True)
        return s_raw, qc, qn, m_st[0, h]

    def update(h, m):
        k = head_slice(k_ref, h)
        a_col = rc_ref[0, h, :, 0:1]
        b_last = rc_ref[0, h, L - 1:L, 1:2]
        a_max = rc_ref[0, h, L - 1:L, 2:3]
        big_n = jnp.maximum(m, a_max)
        decay = jnp.exp(m - big_n)
        w_upd = jnp.exp(a_col - big_n)
        vw = (head_slice(v_ref, h).astype(F32) * w_upd).astype(BF16)
        c_st[0, h] = decay * c_st[0, h] + _dot_tn(vw, k)
        n_st[0, h] = decay * n_st[0, h] + jnp.sum(k.astype(F32) * w_upd, axis=0, keepdims=True)
        m_st[0, h] = b_last + big_n

    def outputs(h, s_raw, qc, qn, m):
        a_row = r_ref[0, h, 0:1, 0:L]
        b_col = rc_ref[0, h, :, 1:2]
        cma_col = rc_ref[0, h, :, 2:3]
        big_m = jnp.maximum(m, cma_col)
        w_intra = jnp.exp(jnp.where(causal, a_row - big_m, NEG_INF))
        w_inter = jnp.exp(m - big_m)
        s = s_raw * w_intra
        num = w_inter * qc + _dot(s.astype(BF16), head_slice(v_ref, h))
        den = w_inter * qn + jnp.sum(s, axis=-1, keepdims=True)
        hh = num / jnp.maximum(jnp.abs(den), jnp.exp(-(b_col + big_m)))
        mu = jnp.mean(hh, axis=-1, keepdims=True)
        hc = hh - mu
        var = jnp.mean(hc * hc, axis=-1, keepdims=True)
        hn_ref[0, :, h * d:(h + 1) * d] = (hc * lax.rsqrt(var + EPS)).astype(BF16)

    pending = {0: scores(0)}
    for h in range(N_HEADS):
        if h + 1 < N_HEADS:
            pending[h + 1] = scores(h + 1)
        s_raw, qc, qn, m = pending.pop(h)
        update(h, m)
        outputs(h, s_raw, qc, qn, m)


def _recurrence(q, k, v, r, rc, state, *, chunk):
    B, T, _ = q.shape
    nc = T // chunk
    d = HEAD_DIM
    has_state = state is not None
    qkv_spec = pl.BlockSpec((1, chunk, INNER), lambda b, c: (b, c, 0))
    lr = r.shape[-1] // nc
    in_specs = [qkv_spec, qkv_spec, qkv_spec,
                pl.BlockSpec((1, N_HEADS, 8, lr), lambda b, c: (b, 0, 0, c)),
                pl.BlockSpec((1, N_HEADS, chunk, 8), lambda b, c: (b, 0, c, 0))]
    args = [q, k, v, r, rc]
    st_specs = [pl.BlockSpec((1, N_HEADS, d, d), lambda b, c: (b, 0, 0, 0)),
                pl.BlockSpec((1, N_HEADS, 1, d), lambda b, c: (b, 0, 0, 0)),
                pl.BlockSpec((1, N_HEADS, 1, 1), lambda b, c: (b, 0, 0, 0))]
    if has_state:
        in_specs += st_specs
        args += list(state)
    return pl.pallas_call(
        functools.partial(_recur_kernel, chunk=chunk, has_state=has_state),
        grid=(B, nc),
        in_specs=in_specs,
        out_specs=[qkv_spec] + st_specs,
        out_shape=[jax.ShapeDtypeStruct((B, T, INNER), BF16),
                   jax.ShapeDtypeStruct((B, N_HEADS, d, d), F32),
                   jax.ShapeDtypeStruct((B, N_HEADS, 1, d), F32),
                   jax.ShapeDtypeStruct((B, N_HEADS, 1, 1), F32)],
        compiler_params=_cparams(("parallel", "arbitrary")),
        name="mlstm_recurrence",
    )(*args)


STATE_ROWS = HEAD_DIM + 16


def _recur_t_kernel(q_ref, k_ref, vt_ref, g_ref, ac_ref, hn_ref, c_out, n_out, m_out, cx_sc, m_sc, *, chunk):
    L = chunk
    d = HEAD_DIM

    @pl.when(pl.program_id(1) == 0)
    def _():
        cx_sc[...] = jnp.zeros_like(cx_sc)
        m_sc[...] = jnp.zeros_like(m_sc)

    si = lax.broadcasted_iota(jnp.int32, (L, L), 0)
    ti = lax.broadcasted_iota(jnp.int32, (L, L), 1)
    causal = si <= ti

    a_rows = g_ref[0, 0:N_HEADS]
    b_rows = g_ref[0, N_HEADS:2 * N_HEADS]
    cma_rows = g_ref[0, 2 * N_HEADS:3 * N_HEADS]
    a_cols = ac_ref[0]

    def scores(h):
        q = q_ref[0, :, h * d:(h + 1) * d]
        st_raw = _dot_nt(k_ref[0, :, h * d:(h + 1) * d], q)
        cq = _dot_nt(cx_sc[h].astype(BF16), q)
        return st_raw, cq, m_sc[h]

    def update(h, m):
        a_row = a_rows[h:h + 1]
        b_last = b_rows[h:h + 1, L - 1:L]
        a_max = cma_rows[h:h + 1, L - 1:L]
        big_n = jnp.maximum(m, a_max)
        decay = jnp.exp(m - big_n)
        w_upd = jnp.exp(a_row - big_n)
        vw = vt_ref[0, h * d:(h + 1) * d, :].astype(F32) * w_upd
        ext = jnp.concatenate([vw, jnp.broadcast_to(w_upd, (STATE_ROWS - d, L))], axis=0).astype(BF16)
        cx_sc[h] = decay * cx_sc[h] + _dot(ext, k_ref[0, :, h * d:(h + 1) * d])
        m_sc[h] = b_last + big_n

    def outputs(h, st_raw, cq, m):
        a_col = a_cols[:, h:h + 1]
        b_row = b_rows[h:h + 1]
        cma_row = cma_rows[h:h + 1]
        big_m = jnp.maximum(m, cma_row)
        w_intra = jnp.exp(jnp.where(causal, a_col - big_m, NEG_INF))
        w_inter = jnp.exp(m - big_m)
        st = st_raw * w_intra
        num = w_inter * cq[:d] + _dot(vt_ref[0, h * d:(h + 1) * d, :], st.astype(BF16))
        den = w_inter * cq[d:d + 1] + jnp.sum(st, axis=0, keepdims=True)
        hh = num * (1.0 / jnp.maximum(jnp.abs(den), jnp.exp(-(b_row + big_m))))
        mu = jnp.mean(hh, axis=0, keepdims=True)
        hc = hh - mu
        var = jnp.mean(hc * hc, axis=0, keepdims=True)
        hn_ref[0, :, h * d:(h + 1) * d] = (hc * lax.rsqrt(var + EPS)).astype(BF16).T

    pending = {0: scores(0)}
    for h in range(N_HEADS):
        if h + 1 < N_HEADS:
            pending[h + 1] = scores(h + 1)
        st_raw, cq, m = pending.pop(h)
        update(h, m)
        outputs(h, st_raw, cq, m)

    @pl.when(pl.program_id(1) == pl.num_programs(1) - 1)
    def _():
        c_out[0] = cx_sc[:, :d, :]
        n_out[0] = cx_sc[:, d:d + 1, :]
        m_out[0] = m_sc[...]


def _recurrence_t(q, k, vt, gate_rows, a_cols, *, chunk):
    B, T, _ = q.shape
    nc = T // chunk
    d = HEAD_DIM
    qk_spec = pl.BlockSpec((1, chunk, INNER), lambda b, c: (b, c, 0))
    return pl.pallas_call(
        functools.partial(_recur_t_kernel, chunk=chunk),
        grid=(B, nc),
        in_specs=[qk_spec, qk_spec,
                  pl.BlockSpec((1, INNER, chunk), lambda b, c: (b, 0, c)),
                  pl.BlockSpec((1, 3 * N_HEADS, chunk), lambda b, c: (b, 0, c)),
                  pl.BlockSpec((1, chunk, 2 * N_HEADS), lambda b, c: (b, c, 0))],
        out_specs=[qk_spec,
                   pl.BlockSpec((1, N_HEADS, d, d), lambda b, c: (b, 0, 0, 0)),
                   pl.BlockSpec((1, N_HEADS, 1, d), lambda b, c: (b, 0, 0, 0)),
                   pl.BlockSpec((1, N_HEADS, 1, 1), lambda b, c: (b, 0, 0, 0))],
        out_shape=[jax.ShapeDtypeStruct((B, T, INNER), BF16),
                   jax.ShapeDtypeStruct((B, N_HEADS, d, d), F32),
                   jax.ShapeDtypeStruct((B, N_HEADS, 1, d), F32),
                   jax.ShapeDtypeStruct((B, N_HEADS, 1, 1), F32)],
        scratch_shapes=[pltpu.VMEM((N_HEADS, STATE_ROWS, d), F32), pltpu.VMEM((N_HEADS, 1, 1), F32)],
        compiler_params=_cparams(("parallel", "arbitrary")),
        name="mlstm_recurrence_t",
    )(q, k, vt, gate_rows, a_cols)


def _mid_kernel(hn_ref, xc_ref, sz_ref, x_ref, cos_ref, sin_ref,
                onw_ref, skip_ref, wout_ref, kvnw_ref, dkv_ref, latw_ref,
                bnw_ref, bwin_ref, qnw_ref, wuq_ref, wuk_ref,
                x1_ref, ckv_ref, kpe_ref, kcat_ref, qcat_ref, gate_ref, *maybe_vt_ref, nb, tt):
    rows = nb * tt
    flat = lambda ref: ref[...].reshape(rows, ref.shape[-1])
    unflat = lambda val: val.reshape(nb, tt, val.shape[-1])
    cos_t = jnp.tile(cos_ref[...], (nb, 1))
    sin_t = jnp.tile(sin_ref[...], (nb, 1))
    y = (flat(hn_ref).astype(F32) * onw_ref[...] + skip_ref[...] * flat(xc_ref).astype(F32)) * flat(sz_ref).astype(F32)
    x1 = flat(x_ref) + _dot(y.astype(BF16), wout_ref[...])
    x1_ref[...] = unflat(x1)

    ukv = _dot(_rms(x1, kvnw_ref[...]).astype(BF16), dkv_ref[...])
    ckv = _rms(ukv[:, :KV_RANK], latw_ref[...])
    kpe = _rope128(ukv[:, KV_RANK:], cos_t, sin_t)
    ckv_ref[...] = unflat(ckv)
    kpe_ref[...] = unflat(kpe[:, :QK_ROPE])
    kcat_ref[:, :, :KV_RANK] = unflat(ckv.astype(BF16))
    kcat_ref[:, :, KV_RANK:] = unflat(kpe.astype(BF16))
    for vt_ref in maybe_vt_ref:
        vt_ref[0, 0] = ckv.astype(BF16).T

    ub = _dot(_rms(x1, bnw_ref[...]).astype(BF16), bwin_ref[...])
    gate_ref[...] = unflat(_silu(ub[:, Q_RANK:]).astype(BF16))
    cq = _rms(ub[:, :Q_RANK], qnw_ref[...]).astype(BF16)
    qq = _dot(cq, wuq_ref[...])
    nope_w = MLA_HEADS * QK_NOPE
    for h in range(MLA_HEADS):
        q_lat = _dot(qq[:, h * QK_NOPE:(h + 1) * QK_NOPE].astype(BF16), wuk_ref[h])
        q_pe = _rope128(qq[:, nope_w + h * LANE:nope_w + (h + 1) * LANE], cos_t, sin_t)
        qcat_ref[:, h, :, :KV_RANK] = unflat(q_lat.astype(BF16))
        qcat_ref[:, h, :, KV_RANK:] = unflat(q_pe.astype(BF16))


def _mid(hn, xc, sz, x, cos_t, sin_t, onw, skip, wout, kvnw, dkv, latw, bnw, bwin, qnw, wuq, wuk, *,
         nb, tt, emit_vt):
    B, T, _ = x.shape
    nt = T // tt
    assert B % nb == 0 and T % tt == 0 and (nb == 1 or not emit_vt)
    tok = lambda w: pl.BlockSpec((nb, tt, w), lambda b, t: (b, t, 0))
    pos = pl.BlockSpec((tt, LANE), lambda b, t: (t, 0))
    vt_specs = [pl.BlockSpec((1, 1, KV_RANK, tt), lambda b, t: (b, t, 0, 0))] if emit_vt else []
    vt_shapes = [jax.ShapeDtypeStruct((B, nt, KV_RANK, tt), BF16)] if emit_vt else []
    return pl.pallas_call(
        functools.partial(_mid_kernel, nb=nb, tt=tt),
        grid=(B // nb, nt),
        in_specs=[tok(INNER), tok(INNER), tok(INNER), tok(D_MODEL), pos, pos,
                  _const_spec((1, INNER)), _const_spec((1, INNER)), _const_spec((INNER, D_MODEL)),
                  _const_spec((1, D_MODEL)), _const_spec((D_MODEL, KCAT)), _const_spec((1, KV_RANK)),
                  _const_spec((1, D_MODEL)), _const_spec((D_MODEL, Q_RANK + GATE_W)), _const_spec((1, Q_RANK)),
                  _const_spec((Q_RANK, 2 * MLA_HEADS * LANE)), _const_spec((MLA_HEADS, QK_NOPE, KV_RANK))],
        out_specs=[tok(D_MODEL), tok(KV_RANK), tok(QK_ROPE), tok(KCAT),
                   pl.BlockSpec((nb, MLA_HEADS, tt, KCAT), lambda b, t: (b, 0, t, 0)),
                   tok(GATE_W)] + vt_specs,
        out_shape=[jax.ShapeDtypeStruct((B, T, D_MODEL), F32),
                   jax.ShapeDtypeStruct((B, T, KV_RANK), F32),
                   jax.ShapeDtypeStruct((B, T, QK_ROPE), F32),
                   jax.ShapeDtypeStruct((B, T, KCAT), BF16),
                   jax.ShapeDtypeStruct((B, MLA_HEADS, T, KCAT), BF16),
                   jax.ShapeDtypeStruct((B, T, GATE_W), BF16)] + vt_shapes,
        compiler_params=_cparams(("parallel", "parallel")),
        name="mid_proj",
    )(hn, xc, sz, x, cos_t, sin_t, onw, skip, wout, kvnw, dkv, latw, bnw, bwin, qnw, wuq, wuk)


def _attn_prompt_kernel(q_ref, k_ref, vt_ref, gate_ref, x1_ref, wuv_ref, wout_ref, fnw_ref, y_ref, acc_sc, *, tq):
    i = pl.program_id(1)
    ahead = 4

    def run(groups, m, l, first):
        m, l = list(m), list(l)
        keys, vals = [], []
        for b0, nb in groups:
            keys.append(k_ref[0, pl.ds(pl.multiple_of(b0 * tq, tq), nb * tq), :])
            vals.append(jnp.concatenate([vt_ref[0, b0 + n] for n in range(nb)], axis=1) if nb > 1
                        else vt_ref[0, b0])
        if first:
            tk = groups[0][1] * tq
            ks = lax.broadcasted_iota(jnp.int32, (tk, tq), 0) - (tk - tq)
            qs = lax.broadcasted_iota(jnp.int32, (tk, tq), 1)
            visible = (ks < 0) | ((ks // MASK_CHUNK) <= (qs // MASK_CHUNK))
        units = [(g, h) for g in range(len(groups)) for h in range(MLA_HEADS)]
        raw = [_dot_nt(keys[g], q_ref[0, h]) for g, h in units[:ahead]]
        for u, (g, h) in enumerate(units):
            if u + ahead < len(units):
                g2, h2 = units[u + ahead]
                raw.append(_dot_nt(keys[g2], q_ref[0, h2]))
            s = raw[u] * MLA_SCALE
            raw[u] = None
            if first and g == 0:
                s = jnp.where(visible, s, NEG_INF)
                m[h] = jnp.max(s, axis=0, keepdims=True)
                p = jnp.exp(s - m[h])
                acc_sc[h] = _dot(vals[g], p.astype(BF16))
                l[h] = jnp.sum(p, axis=0, keepdims=True)
            else:
                m_new = jnp.maximum(m[h], jnp.max(s, axis=0, keepdims=True))
                alpha = jnp.exp(m[h] - m_new)
                p = jnp.exp(s - m_new)
                acc_sc[h] = alpha * acc_sc[h] + _dot(vals[g], p.astype(BF16))
                l[h] = alpha * l[h] + jnp.sum(p, axis=0, keepdims=True)
                m[h] = m_new
        return tuple(m), tuple(l)

    none = (None,) * MLA_HEADS
    pairs = i // 2
    last_pair = (2 * (pairs - 1), 2)
    m, l = lax.switch((i % 2) + 2 * (pairs % 2),
                      [lambda: run([(i, 1)], none, none, True),
                       lambda: run([(i - 1, 2)], none, none, True),
                       lambda: run([(i, 1), last_pair], none, none, True),
                       lambda: run([(i - 1, 2), last_pair], none, none, True)])
    m, l = lax.fori_loop(0, pairs // 2, lambda j, c: run([(4 * j, 2), (4 * j + 2, 2)], c[0], c[1], False), (m, l))
    o_heads = [(acc_sc[h] / l[h]).astype(BF16).T for h in range(MLA_HEADS)]
    y_ref[0] = _final_math(o_heads, gate_ref[0], x1_ref[0], wuv_ref, wout_ref, fnw_ref)


def _attn_prompt(qcat, kcat, vt, gate, x1, wuv, wout, fnw, *, tq):
    B, _, T, _ = qcat.shape
    nk = T // tq
    tok = lambda w: pl.BlockSpec((1, tq, w), lambda b, i: (b, i, 0))
    return pl.pallas_call(
        functools.partial(_attn_prompt_kernel, tq=tq),
        grid=(B, T // tq),
        in_specs=[pl.BlockSpec((1, MLA_HEADS, tq, KCAT), lambda b, i: (b, 0, i, 0)),
                  pl.BlockSpec((1, T, KCAT), lambda b, i: (b, 0, 0)),
                  pl.BlockSpec((1, nk, KV_RANK, tq), lambda b, i: (b, 0, 0, 0)),
                  tok(GATE_W), tok(D_MODEL),
                  _const_spec((MLA_HEADS, KV_RANK, V_HEAD)), _const_spec((GATE_W, D_MODEL)),
                  _const_spec((1, D_MODEL))],
        out_specs=tok(D_MODEL),
        out_shape=jax.ShapeDtypeStruct((B, T, D_MODEL), F32),
        scratch_shapes=[pltpu.VMEM((MLA_HEADS, KV_RANK, tq), F32)],
        compiler_params=_cparams(("parallel", "arbitrary")),
        name="mla_attention_prompt",
    )(qcat, kcat, vt, gate, x1, wuv, wout, fnw)


def _attn_cached_kernel(q_ref, pckv_ref, pkpe_ref, knew_ref, o_ref, *, tq, past):
    rows = MLA_HEADS * tq
    q = q_ref[0].reshape(rows, KCAT)
    ckv = pckv_ref[0].astype(BF16)
    kpe = pkpe_ref[0].astype(BF16)
    knew = knew_ref[0]
    s_old = (_dot_nt(q[:, :KV_RANK], ckv) + _dot_nt(q[:, KV_RANK:KV_RANK + QK_ROPE], kpe)) * MLA_SCALE
    s_new = _dot_nt(q, knew) * MLA_SCALE

    def masked(s, key_pos0):
        r = lax.broadcasted_iota(jnp.int32, s.shape, 0)
        key_pos = key_pos0 + lax.broadcasted_iota(jnp.int32, s.shape, 1)
        q_pos = past + (r & (tq - 1))
        return jnp.where((key_pos // MASK_CHUNK) <= (q_pos // MASK_CHUNK), s, NEG_INF)

    s_old = masked(s_old, 0)
    s_new = masked(s_new, past)
    m = jnp.maximum(jnp.max(s_old, axis=-1, keepdims=True), jnp.max(s_new, axis=-1, keepdims=True))
    p_old = jnp.exp(s_old - m)
    p_new = jnp.exp(s_new - m)
    l = jnp.sum(p_old, axis=-1, keepdims=True) + jnp.sum(p_new, axis=-1, keepdims=True)
    o = (_dot(p_old.astype(BF16), ckv) + _dot(p_new.astype(BF16), knew[:, :KV_RANK])) / l
    o_ref[0] = o.astype(BF16).reshape(MLA_HEADS, tq, KV_RANK)


def _attn_cached(qcat, past_ckv, past_kpe, kcat):
    B, _, tq, _ = qcat.shape
    past = past_ckv.shape[1]
    return pl.pallas_call(
        functools.partial(_attn_cached_kernel, tq=tq, past=past),
        grid=(B,),
        in_specs=[pl.BlockSpec((1, MLA_HEADS, tq, KCAT), lambda b: (b, 0, 0, 0)),
                  pl.BlockSpec((1, past, KV_RANK), lambda b: (b, 0, 0)),
                  pl.BlockSpec((1, past, QK_ROPE), lambda b: (b, 0, 0)),
                  pl.BlockSpec((1, tq, KCAT), lambda b: (b, 0, 0))],
        out_specs=pl.BlockSpec((1, MLA_HEADS, tq, KV_RANK), lambda b: (b, 0, 0, 0)),
        out_shape=jax.ShapeDtypeStruct((B, MLA_HEADS, tq, KV_RANK), BF16),
        compiler_params=_cparams(("parallel",)),
        name="mla_attention_cached",
    )(qcat, past_ckv, past_kpe, kcat)


def _final_math(o_heads, gate, x1, wuv_ref, wout_ref, fnw_ref):
    o = jnp.concatenate([_dot(o_heads[h], wuv_ref[h]) for h in range(MLA_HEADS)], axis=-1)
    og = (o * gate.astype(F32)).astype(BF16)
    x2 = x1 + _dot(og, wout_ref[...])
    return _rms(x2, fnw_ref[...])


def _final_kernel(o_ref, gate_ref, x1_ref, wuv_ref, wout_ref, fnw_ref, y_ref, *, nb, tt):
    rows = nb * tt
    o_heads = [o_ref[:, h].reshape(rows, KV_RANK) for h in range(MLA_HEADS)]
    y = _final_math(o_heads, gate_ref[...].reshape(rows, GATE_W), x1_ref[...].reshape(rows, D_MODEL),
                    wuv_ref, wout_ref, fnw_ref)
    y_ref[...] = y.reshape(nb, tt, D_MODEL)


def _final(o_lat, gate, x1, wuv, wout, fnw, *, nb, tt):
    B, T, _ = x1.shape
    assert B % nb == 0 and T % tt == 0
    tok = lambda w: pl.BlockSpec((nb, tt, w), lambda b, t: (b, t, 0))
    return pl.pallas_call(
        functools.partial(_final_kernel, nb=nb, tt=tt),
        grid=(B // nb, T // tt),
        in_specs=[pl.BlockSpec((nb, MLA_HEADS, tt, KV_RANK), lambda b, t: (b, 0, t, 0)),
                  tok(GATE_W), tok(D_MODEL),
                  _const_spec((MLA_HEADS, KV_RANK, V_HEAD)), _const_spec((GATE_W, D_MODEL)),
                  _const_spec((1, D_MODEL))],
        out_specs=tok(D_MODEL),
        out_shape=jax.ShapeDtypeStruct((B, T, D_MODEL), F32),
        compiler_params=_cparams(("parallel", "parallel")),
        name="mla_final",
    )(o_lat, gate, x1, wuv, wout, fnw)


def _block_diag(w):
    nb = INNER // BD
    rows = w.reshape(nb, BD, QKV_BLOCK)
    tiled = jnp.tile(rows, (1, 1, BD // QKV_BLOCK))
    r = lax.broadcasted_iota(jnp.int32, (BD, BD), 0) // QKV_BLOCK
    c = lax.broadcasted_iota(jnp.int32, (BD, BD), 1) // QKV_BLOCK
    return jnp.where((r == c)[None], tiled, 0.0).astype(BF16)


def _fold_gates(w_head, w_gate):
    wg = w_gate.reshape(INNER // QKV_BLOCK, QKV_BLOCK, 2 * N_HEADS)
    return jnp.einsum("gio,gon->gin", w_head, wg, precision=lax.Precision.HIGHEST).reshape(INNER, 2 * N_HEADS)


def _pad_lanes(w, width):
    return jnp.pad(w, ((0, 0),) * (w.ndim - 1) + ((0, width - w.shape[-1]),))


def _prep_weights(a_norm_w, a_w_in, a_conv_w, a_conv_b, a_w_q, a_w_k, a_w_v, a_w_gates, a_b_gates, a_outnorm_w,
                  a_skip, a_w_out, kv_norm_w, kv_w_dkv, kv_latent_norm_w, kv_w_uk, kv_w_uv,
                  b_norm_w, b_w_in, b_q_norm_w, b_w_uq, b_w_out, final_norm_w):
    l = 0
    wg = a_w_gates[l]
    wgc = _fold_gates(a_w_q[l], wg[:INNER]) + _fold_gates(a_w_k[l], wg[INNER:2 * INNER])
    wgi = _fold_gates(a_w_v[l], wg[2 * INNER:])
    gate_rows = lambda m: jnp.pad(m.T, ((0, GATE_ROWS - 2 * N_HEADS), (0, 0)))
    wuq =b_w_uq[l].reshape(Q_RANK, MLA_HEADS, QK_NOPE + QK_ROPE)
    wuq_nope = wuq[:, :, :QK_NOPE].reshape(Q_RANK, MLA_HEADS * QK_NOPE)
    wuq_rope = jnp.pad(wuq[:, :, QK_NOPE:], ((0, 0), (0, 0), (0, LANE - QK_ROPE))).reshape(Q_RANK, MLA_HEADS * LANE)
    return dict(
        nw=a_norm_w[l][None], win=a_w_in[l].astype(BF16), cw=a_conv_w[l], cb=a_conv_b[l][None],
        bdq=_block_diag(a_w_q[l]), bdk=_block_diag(a_w_k[l]), bdv=_block_diag(a_w_v[l]),
        bdv_t=jnp.swapaxes(_block_diag(a_w_v[l]), 1, 2),
        wgc=_pad_lanes(wgc, LANE).astype(BF16), wgi=_pad_lanes(wgi, LANE).astype(BF16),
        bg=_pad_lanes(a_b_gates[l][None], LANE),
        wgc_t=gate_rows(wgc).astype(BF16), wgi_t=gate_rows(wgi).astype(BF16), bg_t=gate_rows(a_b_gates[l][None]),
        onw=a_outnorm_w[l][None], skip=a_skip[l][None], wout_a=a_w_out[l].astype(BF16),
        kvnw=kv_norm_w[None], dkv=_pad_lanes(kv_w_dkv, KCAT).astype(BF16), latw=kv_latent_norm_w[None],
        bnw=b_norm_w[l][None], bwin=b_w_in[l].astype(BF16), qnw=b_q_norm_w[l][None],
        wuq=jnp.concatenate([wuq_nope, wuq_rope], axis=1).astype(BF16),
        wuk=jnp.transpose(kv_w_uk, (1, 2, 0)).astype(BF16),
        wuv=jnp.transpose(kv_w_uv, (1, 0, 2)).astype(BF16),
        wout_b=b_w_out[l].astype(BF16), fnw=final_norm_w[None],
    )


def _rope_tables(pos):
    half = QK_ROPE // 2
    inv = ROPE_BASE ** (-jnp.arange(half, dtype=F32) / half)
    ang = pos.astype(F32)[:, None] * inv[None, :]
    cos, sin = jnp.cos(ang), jnp.sin(ang)
    zero = jnp.zeros((pos.shape[0], LANE - QK_ROPE), F32)
    return jnp.concatenate([cos, cos, zero], axis=1), jnp.concatenate([-sin, sin, zero], axis=1)


def _trunk(x, pos0, state, conv_buf, cache, w, *, nb, tt, chunk):
    B, T, _ = x.shape
    time_on_lanes = state is None and T % LANE == 0
    gw = (w["bdv_t"], w["wgc_t"], w["wgi_t"], w["bg_t"]) if time_on_lanes else (w["bdv"], w["wgc"], w["wgi"], w["bg"])
    q, k, v, xc, sz, g, conv_tail, *a_cols = _front(x, conv_buf, w["nw"], w["win"], w["cw"], w["cb"], w["bdq"],
                                                    w["bdk"], *gw, nb=nb, tt=tt, v_transposed=time_on_lanes)
    if time_on_lanes:
        assert chunk == tt
        hn, c_new, n_new, m_new = _recurrence_t(q, k, v, g, a_cols[0], chunk=chunk)
    else:
        gt = jnp.transpose(g, (0, 2, 1))
        tp = max(T, LANE)
        if tp != T:
            gt = jnp.pad(gt, ((0, 0), (0, 0), (0, tp - T)))
        r = _gate_prep(gt, chunk=chunk)
        rc = jnp.transpose(r[..., :T], (0, 1, 3, 2))
        if state is not None:
            c0, n0, m0 = state
            state = (c0, n0[:, :, None, :], m0[:, :, None, None])
        hn, c_new, n_new, m_new = _recurrence(q, k, v, r, rc, state, chunk=chunk)

    cos_t, sin_t = _rope_tables(pos0 + jnp.arange(T, dtype=jnp.int32))
    x1, ckv, kpe, kcat, qcat, gate, *vt = _mid(hn, xc, sz, x, cos_t, sin_t, w["onw"], w["skip"], w["wout_a"],
                                               w["kvnw"], w["dkv"], w["latw"], w["bnw"], w["bwin"], w["qnw"],
                                               w["wuq"], w["wuk"], nb=nb, tt=tt, emit_vt=cache is None)
    if cache is None:
        y = _attn_prompt(qcat, kcat, vt[0], gate, x1, w["wuv"], w["wout_b"], w["fnw"], tq=tt)
    else:
        past_ckv, past_kpe = cache
        assert pos0 == past_ckv.shape[1]
        o_lat = _attn_cached(qcat, past_ckv, past_kpe, kcat)
        y = _final(o_lat, gate, x1, w["wuv"], w["wout_b"], w["fnw"], nb=nb, tt=tt)
    return (y, c_new[None], n_new[None, :, :, 0, :], m_new[None, :, :, 0, 0], conv_tail[None], ckv, kpe)


def kernel(x_prompt, x_sample, state_mlstm_C, state_mlstm_n, state_mlstm_m, state_mlstm_conv, cache_mla_ckv,
           cache_mla_kpe, a_norm_w, a_w_in, a_conv_w, a_conv_b, a_w_q, a_w_k, a_w_v, a_w_gates, a_b_gates,
           a_outnorm_w, a_skip, a_w_out, kv_norm_w, kv_w_dkv, kv_latent_norm_w, kv_w_uk, kv_w_uv, b_norm_w, b_w_in,
           b_q_norm_w, b_w_uq, b_w_out, final_norm_w):
    w = _prep_weights(a_norm_w, a_w_in, a_conv_w, a_conv_b, a_w_q, a_w_k, a_w_v, a_w_gates, a_b_gates, a_outnorm_w,
                      a_skip, a_w_out, kv_norm_w, kv_w_dkv, kv_latent_norm_w, kv_w_uk, kv_w_uv,
                      b_norm_w, b_w_in, b_q_norm_w, b_w_uq, b_w_out, final_norm_w)
    bp, tp, _ = x_prompt.shape
    bs, ts, _ = x_sample.shape
    past = cache_mla_ckv.shape[1]
    zero_buf = jnp.zeros((bp, CONV_WIDTH - 1, INNER), F32)
    out_p = _trunk(x_prompt, 0, None, zero_buf, None, w, nb=1, tt=256, chunk=256)
    out_s = _trunk(x_sample, past, (state_mlstm_C[0], state_mlstm_n[0], state_mlstm_m[0]), state_mlstm_conv[0],
                   (cache_mla_ckv, cache_mla_kpe), w, nb=8 if bs % 8 == 0 else 1, tt=ts, chunk=ts)
    return (out_p[0], out_s[0]) + tuple(out_p[1:]) + tuple(out_s[1:])
```
